```python
import math
import jax, jax.numpy as jnp
from jax import lax
import numpy as np

D_MODEL = 1024
BATCH = 2
SEQ = 8192
DEPTH = 1

MEM_LEN = 256
ATTN_HEADS = 8
ATTN_HEAD_DIM = D_MODEL // 16
ATTN_WIDTH = ATTN_HEADS * ATTN_HEAD_DIM
SSM_WIDTH = D_MODEL - ATTN_WIDTH
SSM_GROUP = 16
SSM_GROUPS = SSM_WIDTH // SSM_GROUP
SSM_STATE = 64
MIX_WIDTH = ATTN_WIDTH + SSM_WIDTH
IN_WIDTH = 3 * ATTN_WIDTH + SSM_WIDTH
Q_BLOCK = 128
DT_MIN = 1e-3
DT_MAX = 1e-1
XATTN_HEADS = 4
XATTN_HEAD_DIM = D_MODEL // XATTN_HEADS
PEER_HEADS = 8
PEER_KEYS = 128
PEER_EXPERTS = PEER_KEYS * PEER_KEYS
PEER_TOPK = 16
PEER_KEY_DIM = 128
PEER_QUERY_DIM = 2 * PEER_KEY_DIM
TOKEN_BLOCK = 128
EPS = 1e-6

kernel_name = "hybrid_sba_s5_peer_block"


def rmsnorm(x, g):
    xf = x.astype(jnp.float32)
    y = xf * lax.rsqrt(jnp.mean(xf * xf, axis=-1, keepdims=True) + EPS)
    return (y * g.astype(jnp.float32)).astype(x.dtype)


def stick_breaking_attention(q, k, v):
    b, h, s, dh = q.shape
    nb = s // Q_BLOCK
    scale = dh ** -0.5
    q_blocks = q.reshape(b, h, nb, Q_BLOCK, dh).transpose(2, 0, 1, 3, 4)
    key_pos = jnp.arange(s)

    def block(args):
        q_blk, blk = args
        z = jnp.einsum('bhqd,bhkd->bhqk', q_blk, k, preferred_element_type=jnp.float32) * scale
        q_pos = blk * Q_BLOCK + jnp.arange(Q_BLOCK)
        mask = key_pos[None, :] < q_pos[:, None]
        log_stay = jnp.where(mask, jax.nn.log_sigmoid(-z), 0.0)
        later = lax.cumsum(log_stay, axis=3, reverse=True) - log_stay
        w = jnp.where(mask, jnp.exp(jax.nn.log_sigmoid(z) + later), 0.0)
        return jnp.einsum('bhqk,bhkd->bhqd', w.astype(v.dtype), v)

    out = lax.map(block, (q_blocks, jnp.arange(nb)))
    return out.transpose(1, 2, 0, 3, 4).reshape(b, h, s, dh)


def s5_mixer(u, a_re, a_im, log_dt, b_re, b_im, c_re, c_im, d_skip, w_glu, b_glu):
    bsz, s, _ = u.shape
    f32 = jnp.float32
    uf = u.astype(f32).reshape(bsz, s, SSM_GROUPS, SSM_GROUP)
    lam = lax.complex(a_re.astype(f32), a_im.astype(f32))
    dt = jnp.exp(log_dt.astype(f32))[:, None]
    lam_bar = jnp.exp(lam * dt)
    b_mat = lax.complex(b_re.astype(f32), b_im.astype(f32))
    b_bar = ((lam_bar - 1.0) / lam)[..., None] * b_mat
    bu = jnp.einsum('gpc,bsgc->bsgp', b_bar, uf.astype(jnp.complex64))
    a_seq = jnp.broadcast_to(lam_bar, bu.shape)

    def combine(left, right):
        a_l, b_l = left
        a_r, b_r = right
        return a_r * a_l, a_r * b_l + b_r

    _, states = lax.associative_scan(combine, (a_seq, bu), axis=1)
    c_mat = lax.complex(c_re.astype(f32), c_im.astype(f32))
    y = jnp.real(jnp.einsum('gcp,bsgp->bsgc', c_mat, states)) + d_skip.astype(f32) * uf
    y = jax.nn.gelu(y.reshape(bsz, s, SSM_WIDTH))
    gate = jax.nn.sigmoid(y @ w_glu.astype(f32) + b_glu.astype(f32))
    return (y * gate).astype(u.dtype)


def memory_cross_attention(h, mem_n, w_xq, w_xkv, w_xo):
    b, s, _ = h.shape
    m = mem_n.shape[1]
    q = (h @ w_xq).reshape(b, s, XATTN_HEADS, XATTN_HEAD_DIM)
    kv = (mem_n @ w_xkv).reshape(b, m, 2, XATTN_HEADS, XATTN_HEAD_DIM)
    k, v = kv[:, :, 0], kv[:, :, 1]
    scores = jnp.einsum('bshd,bmhd->bhsm', q, k, preferred_element_type=jnp.float32) * (XATTN_HEAD_DIM ** -0.5)
    p = jax.nn.softmax(scores, axis=-1).astype(v.dtype)
    out = jnp.einsum('bhsm,bmhd->bshd', p, v).reshape(b, s, D_MODEL)
    return out @ w_xo


def peer_ffn(h, w_pq, sub_keys, peer_u, peer_v):
    b, s, d = h.shape
    q = (h @ w_pq).reshape(b, s, PEER_HEADS, 2, PEER_KEY_DIM)
    scores = jnp.einsum('bshid,hikd->bshik', q, sub_keys, preferred_element_type=jnp.float32)
    half_val, half_idx = lax.top_k(scores, PEER_TOPK)
    cand = half_val[..., 0, :, None] + half_val[..., 1, None, :]
    cand = cand.reshape(b, s, PEER_HEADS, PEER_TOPK * PEER_TOPK)
    best_val, best_pos = lax.top_k(cand, PEER_TOPK)
    i1 = jnp.take_along_axis(half_idx[..., 0, :], best_pos // PEER_TOPK, axis=-1)
    i2 = jnp.take_along_axis(half_idx[..., 1, :], best_pos % PEER_TOPK, axis=-1)
    expert = i1 * PEER_KEYS + i2
    gate = jax.nn.softmax(best_val, axis=-1)
    nb = s // TOKEN_BLOCK

    def to_blocks(t):
        return t.reshape(b, nb, TOKEN_BLOCK, *t.shape[2:]).swapaxes(0, 1)

    def block(args):
        h_blk, e_blk, g_blk = args
        u_sel = peer_u[e_blk]
        act = jnp.einsum('btd,bthkd->bthk', h_blk, u_sel, preferred_element_type=jnp.float32)
        w = (g_blk * jax.nn.gelu(act)).astype(h.dtype)
        v_sel = peer_v[e_blk]
        return jnp.einsum('bthk,bthkd->btd', w, v_sel)

    out = lax.map(block, (to_blocks(h), to_blocks(expert), to_blocks(gate)))
    return out.swapaxes(0, 1).reshape(b, s, d)


def setup_inputs(seed: int = 0) -> dict:
    key = jax.random.key(seed)
    ks = jax.random.split(key, 32)
    f32 = jnp.float32
    L = DEPTH

    def nrm(k, shape, scale):
        return jax.random.normal(k, shape, f32) * scale

    def gain(k, shape):
        return 1.0 + 0.02 * jax.random.normal(k, shape, f32)

    n = jnp.arange(SSM_STATE, dtype=f32)
    a_re = -0.5 + 0.01 * jax.random.normal(ks[4], (L, SSM_GROUPS, SSM_STATE), f32)
    a_im = math.pi * n + 0.01 * jax.random.normal(ks[5], (L, SSM_GROUPS, SSM_STATE), f32)
    log_dt = jax.random.uniform(ks[6], (L, SSM_GROUPS), f32, math.log(DT_MIN), math.log(DT_MAX))
    return {
        "x": nrm(ks[0], (BATCH, SEQ, D_MODEL), 1.0),
        "mem": nrm(ks[1], (BATCH, MEM_LEN, D_MODEL), 1.0),
        "g_mix": gain(ks[2], (L, D_MODEL)),
        "w_in": nrm(ks[3], (L, D_MODEL, IN_WIDTH), D_MODEL ** -0.5),
        "a_re": a_re,
        "a_im": a_im,
        "log_dt": log_dt,
        "b_re": nrm(ks[7], (L, SSM_GROUPS, SSM_STATE, SSM_GROUP), (2 * SSM_GROUP) ** -0.5),
        "b_im": nrm(ks[8], (L, SSM_GROUPS, SSM_STATE, SSM_GROUP), (2 * SSM_GROUP) ** -0.5),
        "c_re": nrm(ks[9], (L, SSM_GROUPS, SSM_GROUP, SSM_STATE), (2 * SSM_STATE) ** -0.5),
        "c_im": nrm(ks[10], (L, SSM_GROUPS, SSM_GROUP, SSM_STATE), (2 * SSM_STATE) ** -0.5),
        "d_skip": nrm(ks[11], (L, SSM_GROUPS, SSM_GROUP), 1.0),
        "w_glu": nrm(ks[12], (L, SSM_WIDTH, SSM_WIDTH), SSM_WIDTH ** -0.5),
        "b_glu": nrm(ks[13], (L, SSM_WIDTH), 0.01),
        "g_attn_out": gain(ks[14], (L, ATTN_WIDTH)),
        "g_ssm_out": gain(ks[15], (L, SSM_WIDTH)),
        "w_out": nrm(ks[16], (L, MIX_WIDTH, D_MODEL), MIX_WIDTH ** -0.5),
        "g_xattn": gain(ks[17], (L, D_MODEL)),
        "g_mem": gain(ks[18], (L, D_MODEL)),
        "w_xq": nrm(ks[19], (L, D_MODEL, D_MODEL), D_MODEL ** -0.5),
        "w_xkv": nrm(ks[20], (L, D_MODEL, 2 * D_MODEL), D_MODEL ** -0.5),
        "w_xo": nrm(ks[21], (L, D_MODEL, D_MODEL), D_MODEL ** -0.5),
        "g_ffn": gain(ks[22], (L, D_MODEL)),
        "w_pq": nrm(ks[23], (L, D_MODEL, PEER_HEADS * PEER_QUERY_DIM), D_MODEL ** -0.5),
        "sub_keys": nrm(ks[24], (L, PEER_HEADS, 2, PEER_KEYS, PEER_KEY_DIM), PEER_KEY_DIM ** -0.5),
        "peer_u": nrm(ks[25], (L, PEER_EXPERTS, D_MODEL), D_MODEL ** -0.5),
        "peer_v": nrm(ks[26], (L, PEER_EXPERTS, D_MODEL), PEER_HEADS ** -0.5),
        "g_final": gain(ks[27], (D_MODEL,)),
    }


def reference(x, mem, g_mix, w_in, a_re, a_im, log_dt, b_re, b_im, c_re, c_im, d_skip,
              w_glu, b_glu, g_attn_out, g_ssm_out, w_out, g_xattn, g_mem, w_xq, w_xkv, w_xo,
              g_ffn, w_pq, sub_keys, peer_u, peer_v, g_final):
    b, s, _ = x.shape
    h = x
    for layer in range(DEPTH):
        xn = rmsnorm(h, g_mix[layer])
        proj = xn @ w_in[layer]
        q = proj[..., :ATTN_WIDTH]
        k = proj[..., ATTN_WIDTH:2 * ATTN_WIDTH]
        v = proj[..., 2 * ATTN_WIDTH:3 * ATTN_WIDTH]
        u = proj[..., 3 * ATTN_WIDTH:]

        def heads(t):
            return t.reshape(b, s, ATTN_HEADS, ATTN_HEAD_DIM).transpose(0, 2, 1, 3)

        attn = stick_breaking_attention(heads(q), heads(k), heads(v))
        attn = attn.transpose(0, 2, 1, 3).reshape(b, s, ATTN_WIDTH)
        ssm = s5_mixer(u, a_re[layer], a_im[layer], log_dt[layer], b_re[layer], b_im[layer],
                       c_re[layer], c_im[layer], d_skip[layer], w_glu[layer], b_glu[layer])
        mixed = jnp.concatenate([rmsnorm(attn, g_attn_out[layer]),
                                 rmsnorm(ssm, g_ssm_out[layer])], axis=-1)
        h = h + mixed @ w_out[layer]
        h = h + memory_cross_attention(rmsnorm(h, g_xattn[layer]), rmsnorm(mem, g_mem[layer]),
                                       w_xq[layer], w_xkv[layer], w_xo[layer])
        h = h + peer_ffn(rmsnorm(h, g_ffn[layer]), w_pq[layer], sub_keys[layer],
                         peer_u[layer], peer_v[layer])
    return rmsnorm(h, g_final)
```

```python
import functools
import math

import jax
import jax.numpy as jnp
from jax import lax
from jax.experimental import pallas as pl
from jax.experimental.pallas import tpu as pltpu

F32 = jnp.float32
BF16 = jnp.bfloat16

EPS = 1e-6
D_MODEL = 1024
ATTN_HEADS = 8
HEAD_DIM = 64
ATTN_WIDTH = ATTN_HEADS * HEAD_DIM
SSM_WIDTH = D_MODEL - ATTN_WIDTH
SSM_GROUP = 16
SSM_GROUPS = SSM_WIDTH // SSM_GROUP
SSM_STATE = 64
SSM_CHUNK = 16
SSM_PAIRS = SSM_GROUPS // 2
XATTN_HEADS = 4
XATTN_HEAD_DIM = D_MODEL // XATTN_HEADS
PEER_HEADS = 8
PEER_KEYS = 128
PEER_TOPK = 16
PEER_KEY_DIM = 128

V7X_LANES = 128
V7X_VMEM_LIMIT_BYTES = 48 * 1024 * 1024

NT_DIMS = (((1,), (1,)), ((), ()))


def _params(semantics):
    return pltpu.CompilerParams(dimension_semantics=semantics,
                                vmem_limit_bytes=V7X_VMEM_LIMIT_BYTES)


def _rms(x, g):
    ms = jnp.mean(x * x, axis=-1, keepdims=True)
    return x * lax.rsqrt(ms + EPS) * g


def _dot(a, b):
    return jnp.dot(a, b, preferred_element_type=F32)


def _split(a):
    hi = a.astype(BF16)
    lo = (a - hi.astype(F32)).astype(BF16)
    return hi, lo


def _dot3(a, b):
    ah, al = _split(a)
    bh, bl = _split(b)
    return _dot(ah, bh) + _dot(al, bh) + _dot(ah, bl)


def _inproj_body(x_ref, g_ref, w_ref, q_ref, k_ref, v_ref, u_ref):
    xn = _rms(x_ref[...], g_ref[...]).astype(BF16)
    proj = _dot(xn, w_ref[...])
    aw = ATTN_WIDTH
    q_ref[...] = (proj[:, 0:aw] * (HEAD_DIM ** -0.5)).astype(BF16)
    k_ref[...] = proj[:, aw:2 * aw].astype(BF16)
    v_ref[...] = proj[:, 2 * aw:3 * aw].astype(BF16)
    u_ref[...] = proj[:, 3 * aw:]


def _in_proj(x2, g_mix, w_in, tile):
    n = x2.shape[0]
    tok = lambda w: pl.BlockSpec((tile, w), lambda i: (i, 0))
    full = lambda a: pl.BlockSpec(a.shape, lambda i: (0, 0))
    return pl.pallas_call(
        _inproj_body,
        grid=(n // tile,),
        in_specs=[tok(D_MODEL), full(g_mix), full(w_in)],
        out_specs=[tok(ATTN_WIDTH)] * 3 + [tok(SSM_WIDTH)],
        out_shape=[jax.ShapeDtypeStruct((n, ATTN_WIDTH), BF16)] * 3
        + [jax.ShapeDtypeStruct((n, SSM_WIDTH), F32)],
        compiler_params=_params(("parallel",)),
        name="in_proj",
    )(x2, g_mix, w_in)


def _attn_body(q_ref, k_ref, v_ref, o_ref, acc_ref, car_ref, *, blk):
    i = pl.program_id(2)
    q2 = q_ref[0]
    lane = lax.broadcasted_iota(jnp.int32, (blk, V7X_LANES), 1)
    zero = jnp.zeros_like(q2)
    q_heads = (jnp.where(lane < HEAD_DIM, q2, zero), jnp.where(lane >= HEAD_DIM, q2, zero))
    row = lax.broadcasted_iota(jnp.int32, (blk, blk), 0)
    col = lax.broadcasted_iota(jnp.int32, (blk, blk), 1)
    tri = jnp.where(row > col, 1.0, 0.0).astype(BF16)
    tri2 = jnp.concatenate([tri, tri], axis=0)
    causal = col < row

    acc_ref[...] = jnp.zeros_like(acc_ref)
    car_ref[...] = jnp.zeros_like(car_ref)

    def visit(kb, vb, mask):
        for h in range(2):
            z = lax.dot_general(q_heads[h], kb, NT_DIMS, preferred_element_type=F32)
            ls = jnp.minimum(-z, 0.0) - jnp.log(1.0 + jnp.exp(-jnp.abs(z)))
            if mask is not None:
                ls = jnp.where(mask, ls, 0.0)
            hi, lo = _split(ls)
            inner = _dot(jnp.concatenate([hi, lo], axis=1), tri2)
            carry = car_ref[h]
            later = inner + jnp.concatenate([carry] * (blk // V7X_LANES), axis=1)
            w = jnp.exp(z + ls + later)
            if mask is not None:
                w = jnp.where(mask, w, 0.0)
            acc_ref[h] += _dot(w.astype(BF16), vb)
            total = inner[:, 0:1] + ls[:, 0:1]
            car_ref[h] = carry + jnp.broadcast_to(total, carry.shape)

    start = pl.multiple_of(i * blk, blk)
    visit(k_ref[0, pl.ds(start, blk), :], v_ref[0, pl.ds(start, blk), :], causal)

    def body(t, _):
        s0 = pl.multiple_of((i - 1 - t) * blk, blk)
        visit(k_ref[0, pl.ds(s0, blk), :], v_ref[0, pl.ds(s0, blk), :], None)
        return 0

    lax.fori_loop(0, i, body, 0)
    o_ref[0] = jnp.where(lane < HEAD_DIM, acc_ref[0], acc_ref[1])


def _attention(q, k, v, blk):
    b, s, _ = q.shape
    pairs = ATTN_WIDTH // V7X_LANES
    qspec = pl.BlockSpec((1, blk, V7X_LANES), lambda bi, p, i: (bi, i, p))
    kvspec = pl.BlockSpec((1, s, V7X_LANES), lambda bi, p, i: (bi, 0, p))
    return pl.pallas_call(
        functools.partial(_attn_body, blk=blk),
        grid=(b, pairs, s // blk),
        in_specs=[qspec, kvspec, kvspec],
        out_specs=qspec,
        out_shape=jax.ShapeDtypeStruct((b, s, ATTN_WIDTH), F32),
        scratch_shapes=[pltpu.VMEM((2, blk, V7X_LANES), F32),
                        pltpu.VMEM((2, blk, V7X_LANES), F32)],
        compiler_params=_params(("parallel", "parallel", "arbitrary")),
        name="attn",
    )(q, k, v)


def _ssm_body(u_ref, m_ref, nre_ref, nim_ref, pre_ref, pim_ref, lam_ref, d_ref, y_ref, *, nb, nc):
    u2 = u_ref[0]
    s_re = _dot3(u2, nre_ref[0])
    s_im = _dot3(u2, nim_ref[0])
    row = lax.broadcasted_iota(jnp.int32, (nc, V7X_LANES), 0)
    prev_re, prev_im = [], []
    for b in range(nb):
        xr = s_re[b * nc:(b + 1) * nc]
        xi = s_im[b * nc:(b + 1) * nc]
        d, lvl = 1, 0
        while d < nc:
            ar = lam_ref[0, 2 * lvl:2 * lvl + 1, :]
            ai = lam_ref[0, 2 * lvl + 1:2 * lvl + 2, :]
            keep = row >= d
            sr = jnp.where(keep, pltpu.roll(xr, d, 0), 0.0)
            si = jnp.where(keep, pltpu.roll(xi, d, 0), 0.0)
            xr, xi = xr + ar * sr - ai * si, xi + ar * si + ai * sr
            d, lvl = 2 * d, lvl + 1
        first = row >= 1
        prev_re.append(jnp.where(first, pltpu.roll(xr, 1, 0), 0.0))
        prev_im.append(jnp.where(first, pltpu.roll(xi, 1, 0), 0.0))
    pr = jnp.concatenate(prev_re, axis=0)
    pi = jnp.concatenate(prev_im, axis=0)
    half = SSM_CHUNK * SSM_GROUP
    y = jnp.concatenate([_dot3(u2[:, :half], m_ref[0, 0]), _dot3(u2[:, half:], m_ref[0, 1])], axis=1)
    y = y + _dot3(pr, pre_ref[0]) + _dot3(pi, pim_ref[0])
    y = y + d_ref[0] * u2
    y_ref[0] = jax.nn.gelu(y)


def _ssm_tables(a_re, a_im, log_dt, b_re, b_im, c_re, c_im, d_skip, nc):
    L, G, P, C = SSM_CHUNK, SSM_GROUPS, SSM_STATE, SSM_GROUP
    lam = lax.complex(a_re.astype(F32), a_im.astype(F32))
    dt = jnp.exp(log_dt.astype(F32))[:, None]
    lam_dt = lam * dt
    lam_bar = jnp.exp(lam_dt)
    b_bar = ((lam_bar - 1.0) / lam)[..., None] * lax.complex(b_re.astype(F32), b_im.astype(F32))
    c_mat = lax.complex(c_re.astype(F32), c_im.astype(F32))
    steps = jnp.arange(L + 1, dtype=F32)
    pw = jnp.exp(lam_dt[None] * steps[:, None, None])
    kern = jnp.real(jnp.einsum('gcp,kgp,gpd->kgcd', c_mat, pw[:L], b_bar))
    t_idx = jnp.arange(L)
    lag = t_idx[None, :] - t_idx[:, None]
    toe = jnp.where((lag >= 0)[:, :, None, None, None], kern[jnp.clip(lag, 0, L - 1)], 0.0)
    m = toe.transpose(2, 0, 4, 1, 3).reshape(SSM_PAIRS, 2, L * C, L * C)
    inj = pw[L - 1 - t_idx][:, :, :, None] * b_bar[None]
    inj = inj.transpose(1, 0, 3, 2).reshape(SSM_PAIRS, 2, L * C, P)
    out = c_mat[None] * pw[1:L + 1][:, :, None, :]
    out = out.transpose(1, 3, 0, 2).reshape(SSM_PAIRS, 2, P, L * C)

    def blockdiag(x):
        z = jnp.zeros_like(x[:, 0])
        top = jnp.concatenate([x[:, 0], z], axis=2)
        bot = jnp.concatenate([z, x[:, 1]], axis=2)
        return jnp.concatenate([top, bot], axis=1)

    nre, nim = blockdiag(jnp.real(inj)), blockdiag(jnp.imag(inj))
    pre, pim = blockdiag(jnp.real(out)), blockdiag(-jnp.imag(out))
    levels = []
    d = 1
    while d < nc:
        a = jnp.exp(lam_dt * float(L * d)).reshape(SSM_PAIRS, 2 * P)
        levels += [jnp.real(a), jnp.imag(a)]
        d *= 2
    lam_tab = jnp.stack(levels, axis=1)
    dvec = jnp.broadcast_to(d_skip.astype(F32).reshape(SSM_PAIRS, 2, 1, C), (SSM_PAIRS, 2, L, C))
    dvec = dvec.reshape(SSM_PAIRS, 1, 2 * L * C)
    return m, nre, nim, pre, pim, lam_tab, dvec


def _ssm(u, tables):
    b, s, _ = u.shape
    L, C = SSM_CHUNK, SSM_GROUP
    nc = s // L
    m, nre, nim, pre, pim, lam_tab, dvec = tables
    width = 2 * L * C
    uc = u.reshape(b, nc, L, SSM_PAIRS, 2, C).transpose(3, 0, 1, 4, 2, 5).reshape(SSM_PAIRS, b * nc, width)
    per_pair = lambda a: pl.BlockSpec((1,) + a.shape[1:], lambda p: (p,) + (0,) * (a.ndim - 1))
    y = pl.pallas_call(
        functools.partial(_ssm_body, nb=b, nc=nc),
        grid=(SSM_PAIRS,),
        in_specs=[per_pair(a) for a in (uc, m, nre, nim, pre, pim, lam_tab, dvec)],
        out_specs=per_pair(uc),
        out_shape=jax.ShapeDtypeStruct(uc.shape, F32),
        compiler_params=_params(("parallel",)),
        name="ssm",
    )(uc, m, nre, nim, pre, pim, lam_tab, dvec)
    return y.reshape(SSM_PAIRS, b, nc, 2, L, C).transpose(1, 2, 4, 0, 3, 5).reshape(b, s, SSM_WIDTH)


def _memkv_body(m_ref, g_ref, w_ref, kv_ref):
    mn = _rms(m_ref[...], g_ref[...]).astype(BF16)
    kv_ref[...] = _dot(mn, w_ref[...]).astype(BF16)


def _mem_kv(mem2, g_mem, w_xkv):
    full = lambda a: pl.BlockSpec(a.shape, lambda i: (0, 0))
    return pl.pallas_call(
        _memkv_body,
        grid=(1,),
        in_specs=[full(mem2), full(g_mem), full(w_xkv)],
        out_specs=pl.BlockSpec((mem2.shape[0], 2 * D_MODEL), lambda i: (0, 0)),
        out_shape=jax.ShapeDtypeStruct((mem2.shape[0], 2 * D_MODEL), BF16),
        compiler_params=_params(("arbitrary",)),
        name="mem_kv",
    )(mem2, g_mem, w_xkv)


def _mix_body(x_ref, a_ref, y_ref, gao_ref, gso_ref, wglu_ref, bglu_ref, wout_ref, gx_ref,
              wxq_ref, kv_ref, wxo_ref, gffn_ref, h2_ref, hn_ref):
    an = _rms(a_ref[0], gao_ref[...]).astype(BF16)
    y = y_ref[0]
    gate = 1.0 / (1.0 + jnp.exp(-(_dot(y.astype(BF16), wglu_ref[...]) + bglu_ref[...])))
    sn = _rms(y * gate, gso_ref[...]).astype(BF16)
    h1 = x_ref[0] + _dot(jnp.concatenate([an, sn], axis=1), wout_ref[...])
    qx = _dot(_rms(h1, gx_ref[...]).astype(BF16), wxq_ref[...])
    heads = []
    for h in range(XATTN_HEADS):
        lo, hi = h * XATTN_HEAD_DIM, (h + 1) * XATTN_HEAD_DIM
        km = kv_ref[0, :, lo:hi]
        vm = kv_ref[0, :, D_MODEL + lo:D_MODEL + hi]
        sc = lax.dot_general(qx[:, lo:hi].astype(BF16), km, NT_DIMS,
                             preferred_element_type=F32) * (XATTN_HEAD_DIM ** -0.5)
        e = jnp.exp(sc - jnp.max(sc, axis=-1, keepdims=True))
        p = e / jnp.sum(e, axis=-1, keepdims=True)
        heads.append(_dot(p.astype(BF16), vm).astype(BF16))
    h2 = h1 + _dot(jnp.concatenate(heads, axis=1), wxo_ref[...])
    h2_ref[0] = h2
    hn_ref[0] = _rms(h2, gffn_ref[...]).astype(BF16)


def _mix(x, attn, yssm, kv, g_attn_out, g_ssm_out, w_glu, b_glu, w_out, g_xattn, w_xq, w_xo, g_ffn, tile):
    b, s, _ = x.shape
    tok = lambda w: pl.BlockSpec((1, tile, w), lambda bi, i: (bi, i, 0))
    full = lambda a: pl.BlockSpec(a.shape, lambda bi, i: (0,) * a.ndim)
    kvspec = pl.BlockSpec((1,) + kv.shape[1:], lambda bi, i: (bi, 0, 0))
    return pl.pallas_call(
        _mix_body,
        grid=(b, s // tile),
        in_specs=[tok(D_MODEL), tok(ATTN_WIDTH), tok(SSM_WIDTH), full(g_attn_out), full(g_ssm_out),
                  full(w_glu), full(b_glu), full(w_out), full(g_xattn), full(w_xq), kvspec,
                  full(w_xo), full(g_ffn)],
        out_specs=[tok(D_MODEL), tok(D_MODEL)],
        out_shape=[jax.ShapeDtypeStruct((b, s, D_MODEL), F32),
                   jax.ShapeDtypeStruct((b, s, D_MODEL), BF16)],
        compiler_params=_params(("parallel", "parallel")),
        name="mix",
    )(x, attn, yssm, g_attn_out, g_ssm_out, w_glu, b_glu, w_out, g_xattn, w_xq, kv, w_xo, g_ffn)


def _topk_rows(s, k, val_ref, idx_ref):
    n = s.shape[0]
    rowid = lax.broadcasted_iota(jnp.int32, s.shape, 0).astype(F32)
    for r in range(k):
        m = jnp.max(s, axis=0, keepdims=True)
        ix = jnp.min(jnp.where(s == m, rowid, float(n)), axis=0, keepdims=True)
        val_ref[r:r + 1, :] = m
        idx_ref[r:r + 1, :] = ix
        s = jnp.where(rowid == ix, -jnp.inf, s)
    return s


def _route_body(hn_ref, wq_ref, sk_ref, an_ref, nn_ref, bn_ref, rk_ref,
                v1_ref, i1_ref, v2_ref, i2_ref, cand_ref, bv_ref, bi_ref):
    k = PEER_TOPK
    qt = lax.dot_general(wq_ref[...], hn_ref[...], NT_DIMS, preferred_element_type=F32)
    t = qt.shape[1]
    keyrow = lax.broadcasted_iota(jnp.int32, (PEER_KEYS, t), 0).astype(F32)
    for h in range(PEER_HEADS):
        for half, (vr, ir) in enumerate(((v1_ref, i1_ref), (v2_ref, i2_ref))):
            j = 2 * h + half
            qh = qt[j * PEER_KEY_DIM:(j + 1) * PEER_KEY_DIM, :].astype(BF16)
            _topk_rows(_dot(sk_ref[j], qh), k, vr, ir)
        v1, v2 = v1_ref[...], v2_ref[...]
        for a in range(k):
            cand_ref[a * k:(a + 1) * k, :] = v1[a:a + 1, :] + v2
        left = _topk_rows(cand_ref[...], k, bv_ref, bi_ref)
        sel = jnp.where(left == -jnp.inf, 1.0, 0.0)
        e1 = jnp.exp(v1 - v1[0:1, :])
        e2 = jnp.exp(v2 - v2[0:1, :])
        counts, z = [], jnp.zeros((1, t), F32)
        for a in range(k):
            sel_a = sel[a * k:(a + 1) * k, :]
            counts.append(jnp.sum(sel_a, axis=0, keepdims=True))
            z = z + e1[a:a + 1, :] * jnp.sum(sel_a * e2, axis=0, keepdims=True)
        i1, i2 = i1_ref[...], i2_ref[...]
        an = jnp.zeros((PEER_KEYS, t), F32)
        nn = jnp.zeros((PEER_KEYS, t), F32)
        bn = jnp.zeros((PEER_KEYS, t), F32)
        rk = jnp.full((PEER_KEYS, t), float(k), F32)
        for a in range(k):
            hit1 = keyrow == i1[a:a + 1, :]
            an = jnp.where(hit1, e1[a:a + 1, :] / z, an)
            nn = jnp.where(hit1, counts[a], nn)
            hit2 = keyrow == i2[a:a + 1, :]
            bn = jnp.where(hit2, e2[a:a + 1, :], bn)
            rk = jnp.where(hit2, float(a), rk)
        an_ref[h] = an
        nn_ref[h] = nn
        bn_ref[h] = bn
        rk_ref[h] = rk


def _route(hn2, wq_t, sk, tile):
    n = hn2.shape[0]
    k = PEER_TOPK
    tab = pl.BlockSpec((PEER_HEADS, PEER_KEYS, tile), lambda i: (0, 0, i))
    tab_shape = jax.ShapeDtypeStruct((PEER_HEADS, PEER_KEYS, n), F32)
    return pl.pallas_call(
        _route_body,
        grid=(n // tile,),
        in_specs=[pl.BlockSpec((tile, D_MODEL), lambda i: (i, 0)),
                  pl.BlockSpec(wq_t.shape, lambda i: (0, 0)),
                  pl.BlockSpec(sk.shape, lambda i: (0, 0, 0))],
        out_specs=[tab] * 4,
        out_shape=[tab_shape] * 4,
        scratch_shapes=[pltpu.VMEM((k, tile), F32)] * 4
        + [pltpu.VMEM((k * k, tile), F32), pltpu.VMEM((k, tile), F32), pltpu.VMEM((k, tile), F32)],
        compiler_params=_params(("parallel",)),
        name="route",
    )(hn2, wq_t, sk)


def _peer_body(hn_ref, u_ref, vt_ref, an_ref, nn_ref, bn_ref, rk_ref, h2_ref, gf_ref, o_ref, acc_ref,
               *, rows):
    e = pl.program_id(1)

    @pl.when(e == 0)
    def _():
        acc_ref[...] = jnp.zeros_like(acc_ref)

    act_t = lax.dot_general(u_ref[...], hn_ref[...], NT_DIMS, preferred_element_type=F32)
    pieces = []
    for rr in range(rows):
        r = e * rows + rr
        gate = jnp.zeros((PEER_KEYS, act_t.shape[1]), F32)
        for h in range(PEER_HEADS):
            a_row = an_ref[h, pl.ds(r, 1), :]
            n_row = nn_ref[h, pl.ds(r, 1), :]
            gate = gate + a_row * jnp.where(rk_ref[h] < n_row, bn_ref[h], 0.0)
        act = act_t[rr * PEER_KEYS:(rr + 1) * PEER_KEYS, :]
        pieces.append((jax.nn.gelu(act) * gate).astype(BF16))
    acc_ref[...] += _dot(vt_ref[...], jnp.concatenate(pieces, axis=0))

    @pl.when(e == pl.num_programs(1) - 1)
    def _():
        h3 = h2_ref[...] + acc_ref[...].T
        o_ref[...] = _rms(h3, gf_ref[...])


def _peer(hn2, pu, pvt, tables, h2, g_final, tile, rows):
    n = hn2.shape[0]
    te = rows * PEER_KEYS
    n_exp = pu.shape[0]
    tok = pl.BlockSpec((tile, D_MODEL), lambda i, e: (i, 0))
    tab = pl.BlockSpec((PEER_HEADS, PEER_KEYS, tile), lambda i, e: (0, 0, i))
    return pl.pallas_call(
        functools.partial(_peer_body, rows=rows),
        grid=(n // tile, n_exp // te),
        in_specs=[tok,
                  pl.BlockSpec((te, D_MODEL), lambda i, e: (e, 0)),
                  pl.BlockSpec((D_MODEL, te), lambda i, e: (0, e)),
                  tab, tab, tab, tab, tok,
                  pl.BlockSpec(g_final.shape, lambda i, e: (0, 0))],
        out_specs=tok,
        out_shape=jax.ShapeDtypeStruct((n, D_MODEL), F32),
        scratch_shapes=[pltpu.VMEM((D_MODEL, tile), F32)],
        compiler_params=_params(("parallel", "arbitrary")),
        name="peer",
    )(hn2, pu, pvt, *tables, h2, g_final)


TOKEN_TILE = 512
ATTN_BLOCK = 256
PEER_ROWS = 4


def kernel(x, mem, g_mix, w_in, a_re, a_im, log_dt, b_re, b_im, c_re, c_im, d_skip, w_glu, b_glu,
           g_attn_out, g_ssm_out, w_out, g_xattn, g_mem, w_xq, w_xkv, w_xo, g_ffn, w_pq, sub_keys,
           peer_u, peer_v, g_final):
    b, s, d = x.shape
    n = b * s
    depth = g_mix.shape[0]
    row = lambda v: v.reshape(1, -1).astype(F32)
    h = x
    for layer in range(depth):
        q, k, v, u = _in_proj(h.reshape(n, d), row(g_mix[layer]), w_in[layer].astype(BF16), TOKEN_TILE)
        attn = _attention(q.reshape(b, s, -1), k.reshape(b, s, -1), v.reshape(b, s, -1), ATTN_BLOCK)
        tables = _ssm_tables(a_re[layer], a_im[layer], log_dt[layer], b_re[layer], b_im[layer],
                             c_re[layer], c_im[layer], d_skip[layer], s // SSM_CHUNK)
        yssm = _ssm(u.reshape(b, s, -1), tables)
        kv = _mem_kv(mem.reshape(-1, d), row(g_mem[layer]), w_xkv[layer].astype(BF16))
        h2, hn = _mix(h, attn, yssm, kv.reshape(b, -1, 2 * d), row(g_attn_out[layer]), row(g_ssm_out[layer]),
                      w_glu[layer].astype(BF16), row(b_glu[layer]), w_out[layer].astype(BF16),
                      row(g_xattn[layer]), w_xq[layer].astype(BF16), w_xo[layer].astype(BF16),
                      row(g_ffn[layer]), TOKEN_TILE)
        hn2 = hn.reshape(n, d)
        sk = sub_keys[layer].reshape(2 * PEER_HEADS, PEER_KEYS, PEER_KEY_DIM).astype(BF16)
        routing = _route(hn2, w_pq[layer].T.astype(BF16), sk, TOKEN_TILE)
        assert depth == 1
        out = _peer(hn2, peer_u[layer].astype(BF16), peer_v[layer].T.astype(BF16), routing,
                    h2.reshape(n, d), row(g_final), TOKEN_TILE, PEER_ROWS)
        h = out.reshape(b, s, d)
    return h
```

```python
import functools
import math

import jax
import jax.numpy as jnp
from jax import lax
from jax.experimental import pallas as pl
from jax.experimental.pallas import tpu as pltpu

F32 = jnp.float32
BF16 = jnp.bfloat16

EPS = 1e-6
D_MODEL = 1024
ATTN_HEADS = 8
HEAD_DIM = 64
ATTN_WIDTH = ATTN_HEADS * HEAD_DIM
SSM_WIDTH = D_MODEL - ATTN_WIDTH
SSM_GROUP = 16
SSM_GROUPS = SSM_WIDTH // SSM_GROUP
SSM_STATE = 64
SSM_CHUNK = 16
SSM_PAIRS = SSM_GROUPS // 2
XATTN_HEADS = 4
XATTN_HEAD_DIM = D_MODEL // XATTN_HEADS
PEER_HEADS = 8
PEER_KEYS = 128
PEER_TOPK = 16
PEER_KEY_DIM = 128

V7X_LANES = 128
BF16_SUBLANES = 16
V7X_VMEM_LIMIT_BYTES = 48 * 1024 * 1024

NT_DIMS = (((1,), (1,)), ((), ()))


def _params(semantics):
    return pltpu.CompilerParams(dimension_semantics=semantics,
                                vmem_limit_bytes=V7X_VMEM_LIMIT_BYTES)


def _rms(x, g):
    ms = jnp.mean(x * x, axis=-1, keepdims=True)
    return x * lax.rsqrt(ms + EPS) * g


def _dot(a, b):
    return jnp.dot(a, b, preferred_element_type=F32)


def _split(a):
    hi = a.astype(BF16)
    lo = (a - hi.astype(F32)).astype(BF16)
    return hi, lo


def _dot3(a, b):
    ah, al = _split(a)
    bh, bl = _split(b)
    return _dot(ah, bh) + _dot(al, bh) + _dot(ah, bl)


def _inproj_body(x_ref, g_ref, w_ref, q_ref, k_ref, v_ref, u_ref):
    xn = _rms(x_ref[...], g_ref[...]).astype(BF16)
    proj = _dot(xn, w_ref[...])
    aw = ATTN_WIDTH
    q_ref[...] = (proj[:, 0:aw] * (HEAD_DIM ** -0.5)).astype(BF16)
    k_ref[...] = proj[:, aw:2 * aw].astype(BF16)
    v_ref[...] = proj[:, 2 * aw:3 * aw].astype(BF16)
    u_ref[...] = proj[:, 3 * aw:]


def _in_proj(x2, g_mix, w_in, tile):
    n = x2.shape[0]
    tok = lambda w: pl.BlockSpec((tile, w), lambda i: (i, 0))
    full = lambda a: pl.BlockSpec(a.shape, lambda i: (0, 0))
    return pl.pallas_call(
        _inproj_body,
        grid=(n // tile,),
        in_specs=[tok(D_MODEL), full(g_mix), full(w_in)],
        out_specs=[tok(ATTN_WIDTH)] * 3 + [tok(SSM_WIDTH)],
        out_shape=[jax.ShapeDtypeStruct((n, ATTN_WIDTH), BF16)] * 3
        + [jax.ShapeDtypeStruct((n, SSM_WIDTH), F32)],
        compiler_params=_params(("parallel",)),
        name="in_proj",
    )(x2, g_mix, w_in)


def _attn_body(q_ref, k_ref, v_ref, o_ref, acc_ref, car_ref, *, blk, heads):
    i = pl.program_id(2)
    width = heads * HEAD_DIM
    qa = q_ref[0]
    lane = lax.broadcasted_iota(jnp.int32, (blk, width), 1)
    in_head = [(lane >= h * HEAD_DIM) & (lane < (h + 1) * HEAD_DIM) for h in range(heads)]
    zero = jnp.zeros_like(qa)
    q_heads = [jnp.where(m, qa, zero) for m in in_head]
    row = lax.broadcasted_iota(jnp.int32, (blk, blk), 0)
    col = lax.broadcasted_iota(jnp.int32, (blk, blk), 1)
    tri = jnp.where(row > col, 1.0, 0.0).astype(BF16)
    causal = col < row

    acc_ref[...] = jnp.zeros_like(acc_ref)
    car_ref[...] = jnp.zeros_like(car_ref)

    def visit(kb, vb, mask):
        ws, vs = [], []
        for h in range(heads):
            z = lax.dot_general(q_heads[h], kb, NT_DIMS, preferred_element_type=F32)
            sp = jnp.maximum(z, 0.0) + jnp.log(1.0 + jnp.exp(-jnp.abs(z)))
            if mask is not None:
                sp = jnp.where(mask, sp, 0.0)
            inner = _dot(sp.astype(BF16), tri)
            carry = car_ref[h]
            later = inner + jnp.concatenate([carry] * (blk // V7X_LANES), axis=1)
            w = jnp.exp(z - sp - later)
            if mask is not None:
                w = jnp.where(mask, w, 0.0)
            ws.append(w.astype(BF16))
            vs.append(jnp.where(in_head[h], vb, jnp.zeros_like(vb)))
            total = inner[:, 0:1] + sp[:, 0:1]
            car_ref[h] = carry + jnp.broadcast_to(total, carry.shape)
        acc_ref[...] += _dot(jnp.concatenate(ws, axis=1), jnp.concatenate(vs, axis=0))

    start = pl.multiple_of(i * blk, blk)
    visit(k_ref[0, pl.ds(start, blk), :], v_ref[0, pl.ds(start, blk), :], causal)

    def body(t, _):
        s0 = pl.multiple_of((i - 1 - t) * blk, blk)
        visit(k_ref[0, pl.ds(s0, blk), :], v_ref[0, pl.ds(s0, blk), :], None)
        return 0

    lax.fori_loop(0, i, body, 0)
    o_ref[0] = acc_ref[...]


def _attention(q, k, v, blk, heads):
    b, s, _ = q.shape
    width = heads * HEAD_DIM
    assert blk == width, "the value masks reuse the (blk, width) lane iota"
    qspec = pl.BlockSpec((1, blk, width), lambda bi, p, i: (bi, i, p))
    kvspec = pl.BlockSpec((1, s, width), lambda bi, p, i: (bi, 0, p))
    return pl.pallas_call(
        functools.partial(_attn_body, blk=blk, heads=heads),
        grid=(b, ATTN_WIDTH // width, s // blk),
        in_specs=[qspec, kvspec, kvspec],
        out_specs=qspec,
        out_shape=jax.ShapeDtypeStruct((b, s, ATTN_WIDTH), F32),
        scratch_shapes=[pltpu.VMEM((blk, width), F32),
                        pltpu.VMEM((heads, blk, V7X_LANES), F32)],
        compiler_params=_params(("parallel", "parallel", "arbitrary")),
        name="attn",
    )(q, k, v)


def _ssm_body(u_ref, m_ref, nre_ref, nim_ref, pre_ref, pim_ref, lam_ref, d_ref, y_ref, *, nb, nc):
    u2 = u_ref[0]
    s_re = _dot3(u2, nre_ref[0])
    s_im = _dot3(u2, nim_ref[0])
    row = lax.broadcasted_iota(jnp.int32, (nc, V7X_LANES), 0)
    prev_re, prev_im = [], []
    for b in range(nb):
        xr = s_re[b * nc:(b + 1) * nc]
        xi = s_im[b * nc:(b + 1) * nc]
        d, lvl = 1, 0
        while d < nc:
            ar = lam_ref[0, 2 * lvl:2 * lvl + 1, :]
            ai = lam_ref[0, 2 * lvl + 1:2 * lvl + 2, :]
            keep = row >= d
            sr = jnp.where(keep, pltpu.roll(xr, d, 0), 0.0)
            si = jnp.where(keep, pltpu.roll(xi, d, 0), 0.0)
            xr, xi = xr + ar * sr - ai * si, xi + ar * si + ai * sr
            d, lvl = 2 * d, lvl + 1
        first = row >= 1
        prev_re.append(jnp.where(first, pltpu.roll(xr, 1, 0), 0.0))
        prev_im.append(jnp.where(first, pltpu.roll(xi, 1, 0), 0.0))
    pr = jnp.concatenate(prev_re, axis=0)
    pi = jnp.concatenate(prev_im, axis=0)
    half = SSM_CHUNK * SSM_GROUP
    y = jnp.concatenate([_dot3(u2[:, :half], m_ref[0, 0]), _dot3(u2[:, half:], m_ref[0, 1])], axis=1)
    y = y + _dot3(pr, pre_ref[0]) + _dot3(pi, pim_ref[0])
    y = y + d_ref[0] * u2
    y_ref[0] = jax.nn.gelu(y)


def _ssm_tables(a_re, a_im, log_dt, b_re, b_im, c_re, c_im, d_skip, nc):
    L, G, P, C = SSM_CHUNK, SSM_GROUPS, SSM_STATE, SSM_GROUP
    lam = lax.complex(a_re.astype(F32), a_im.astype(F32))
    dt = jnp.exp(log_dt.astype(F32))[:, None]
    lam_dt = lam * dt
    lam_bar = jnp.exp(lam_dt)
    b_bar = ((lam_bar - 1.0) / lam)[..., None] * lax.complex(b_re.astype(F32), b_im.astype(F32))
    c_mat = lax.complex(c_re.astype(F32), c_im.astype(F32))
    steps = jnp.arange(L + 1, dtype=F32)
    pw = jnp.exp(lam_dt[None] * steps[:, None, None])
    kern = jnp.real(jnp.einsum('gcp,kgp,gpd->kgcd', c_mat, pw[:L], b_bar))
    t_idx = jnp.arange(L)
    lag = t_idx[None, :] - t_idx[:, None]
    toe = jnp.where((lag >= 0)[:, :, None, None, None], kern[jnp.clip(lag, 0, L - 1)], 0.0)
    m = toe.transpose(2, 0, 4, 1, 3).reshape(SSM_PAIRS, 2, L * C, L * C)
    inj = pw[L - 1 - t_idx][:, :, :, None] * b_bar[None]
    inj = inj.transpose(1, 0, 3, 2).reshape(SSM_PAIRS, 2, L * C, P)
    out = c_mat[None] * pw[1:L + 1][:, :, None, :]
    out = out.transpose(1, 3, 0, 2).reshape(SSM_PAIRS, 2, P, L * C)

    def blockdiag(x):
        z = jnp.zeros_like(x[:, 0])
        top = jnp.concatenate([x[:, 0], z], axis=2)
        bot = jnp.concatenate([z, x[:, 1]], axis=2)
        return jnp.concatenate([top, bot], axis=1)

    nre, nim = blockdiag(jnp.real(inj)), blockdiag(jnp.imag(inj))
    pre, pim = blockdiag(jnp.real(out)), blockdiag(-jnp.imag(out))
    levels = []
    d = 1
    while d < nc:
        a = jnp.exp(lam_dt * float(L * d)).reshape(SSM_PAIRS, 2 * P)
        levels += [jnp.real(a), jnp.imag(a)]
        d *= 2
    lam_tab = jnp.stack(levels, axis=1)
    dvec = jnp.broadcast_to(d_skip.astype(F32).reshape(SSM_PAIRS, 2, 1, C), (SSM_PAIRS, 2, L, C))
    dvec = dvec.reshape(SSM_PAIRS, 1, 2 * L * C)
    return m, nre, nim, pre, pim, lam_tab, dvec


def _ssm(u, tables):
    b, s, _ = u.shape
    L, C = SSM_CHUNK, SSM_GROUP
    nc = s // L
    m, nre, nim, pre, pim, lam_tab, dvec = tables
    width = 2 * L * C
    uc = u.reshape(b, nc, L, SSM_PAIRS, 2, C).transpose(3, 0, 1, 4, 2, 5).reshape(SSM_PAIRS, b * nc, width)
    per_pair = lambda a: pl.BlockSpec((1,) + a.shape[1:], lambda p: (p,) + (0,) * (a.ndim - 1))
    y = pl.pallas_call(
        functools.partial(_ssm_body, nb=b, nc=nc),
        grid=(SSM_PAIRS,),
        in_specs=[per_pair(a) for a in (uc, m, nre, nim, pre, pim, lam_tab, dvec)],
        out_specs=per_pair(uc),
        out_shape=jax.ShapeDtypeStruct(uc.shape, F32),
        compiler_params=_params(("parallel",)),
        name="ssm",
    )(uc, m, nre, nim, pre, pim, lam_tab, dvec)
    return y.reshape(SSM_PAIRS, b, nc, 2, L, C).transpose(1, 2, 4, 0, 3, 5).reshape(b, s, SSM_WIDTH)


def _memkv_body(m_ref, g_ref, w_ref, kv_ref):
    mn = _rms(m_ref[...], g_ref[...]).astype(BF16)
    kv_ref[...] = _dot(mn, w_ref[...]).astype(BF16)


def _mem_kv(mem2, g_mem, w_xkv):
    full = lambda a: pl.BlockSpec(a.shape, lambda i: (0, 0))
    return pl.pallas_call(
        _memkv_body,
        grid=(1,),
        in_specs=[full(mem2), full(g_mem), full(w_xkv)],
        out_specs=pl.BlockSpec((mem2.shape[0], 2 * D_MODEL), lambda i: (0, 0)),
        out_shape=jax.ShapeDtypeStruct((mem2.shape[0], 2 * D_MODEL), BF16),
        compiler_params=_params(("arbitrary",)),
        name="mem_kv",
    )(mem2, g_mem, w_xkv)


def _mix_body(x_ref, a_ref, y_ref, gao_ref, gso_ref, wglu_ref, bglu_ref, wout_ref, gx_ref,
              wxq_ref, kv_ref, wxo_ref, gffn_ref, h2_ref, hn_ref):
    an = _rms(a_ref[0], gao_ref[...]).astype(BF16)
    y = y_ref[0]
    gate = 1.0 / (1.0 + jnp.exp(-(_dot(y.astype(BF16), wglu_ref[...]) + bglu_ref[...])))
    sn = _rms(y * gate, gso_ref[...]).astype(BF16)
    h1 = x_ref[0] + _dot(jnp.concatenate([an, sn], axis=1), wout_ref[...])
    qx = _dot(_rms(h1, gx_ref[...]).astype(BF16), wxq_ref[...])
    heads = []
    for h in range(XATTN_HEADS):
        lo, hi = h * XATTN_HEAD_DIM, (h + 1) * XATTN_HEAD_DIM
        km = kv_ref[0, :, lo:hi]
        vm = kv_ref[0, :, D_MODEL + lo:D_MODEL + hi]
        sc = lax.dot_general(qx[:, lo:hi].astype(BF16), km, NT_DIMS,
                             preferred_element_type=F32) * (XATTN_HEAD_DIM ** -0.5)
        e = jnp.exp(sc - jnp.max(sc, axis=-1, keepdims=True))
        p = e / jnp.sum(e, axis=-1, keepdims=True)
        heads.append(_dot(p.astype(BF16), vm).astype(BF16))
    h2 = h1 + _dot(jnp.concatenate(heads, axis=1), wxo_ref[...])
    h2_ref[0] = h2
    hn_ref[0] = _rms(h2, gffn_ref[...]).astype(BF16)


def _mix(x, attn, yssm, kv, g_attn_out, g_ssm_out, w_glu, b_glu, w_out, g_xattn, w_xq, w_xo, g_ffn, tile):
    b, s, _ = x.shape
    tok = lambda w: pl.BlockSpec((1, tile, w), lambda bi, i: (bi, i, 0))
    full = lambda a: pl.BlockSpec(a.shape, lambda bi, i: (0,) * a.ndim)
    kvspec = pl.BlockSpec((1,) + kv.shape[1:], lambda bi, i: (bi, 0, 0))
    return pl.pallas_call(
        _mix_body,
        grid=(b, s // tile),
        in_specs=[tok(D_MODEL), tok(ATTN_WIDTH), tok(SSM_WIDTH), full(g_attn_out), full(g_ssm_out),
                  full(w_glu), full(b_glu), full(w_out), full(g_xattn), full(w_xq), kvspec,
                  full(w_xo), full(g_ffn)],
        out_specs=[tok(D_MODEL), tok(D_MODEL)],
        out_shape=[jax.ShapeDtypeStruct((b, s, D_MODEL), F32),
                   jax.ShapeDtypeStruct((b, s, D_MODEL), BF16)],
        compiler_params=_params(("parallel", "parallel")),
        name="mix",
    )(x, attn, yssm, g_attn_out, g_ssm_out, w_glu, b_glu, w_out, g_xattn, w_xq, kv, w_xo, g_ffn)


def _topk_rows(s, k, val_ref=None, idx_ref=None):
    n = s.shape[0]
    rowid = lax.broadcasted_iota(jnp.int32, s.shape, 0).astype(F32)
    for r in range(k):
        m = jnp.max(s, axis=0, keepdims=True)
        ix = jnp.min(jnp.where(s == m, rowid, float(n)), axis=0, keepdims=True)
        if val_ref is not None:
            val_ref[r:r + 1, :] = m
            idx_ref[r:r + 1, :] = ix
        s = jnp.where(rowid == ix, -jnp.inf, s)
    return s


_CAND_COLS = [PEER_TOPK // (a + 1) for a in range(PEER_TOPK)]
_CAND_OFFS = [sum(_CAND_COLS[:a]) for a in range(PEER_TOPK)]
_CAND_ROWS = -(-sum(_CAND_COLS) // 8) * 8


def _route_body(hn_ref, wq_ref, sk_ref, an_ref, nn_ref, bn_ref, rk_ref,
                v1_ref, i1_ref, v2_ref, i2_ref, cand_ref):
    k = PEER_TOPK
    qt = lax.dot_general(wq_ref[...], hn_ref[...], NT_DIMS, preferred_element_type=F32)
    t = qt.shape[1]
    keyrow = lax.broadcasted_iota(jnp.int32, (PEER_KEYS, t), 0).astype(F32)
    for h in range(PEER_HEADS):
        for half, (vr, ir) in enumerate(((v1_ref, i1_ref), (v2_ref, i2_ref))):
            j = 2 * h + half
            qh = qt[j * PEER_KEY_DIM:(j + 1) * PEER_KEY_DIM, :].astype(BF16)
            _topk_rows(_dot(sk_ref[j], qh), k, vr, ir)
        v1, v2 = v1_ref[...], v2_ref[...]
        cand_ref[...] = jnp.full(cand_ref.shape, -jnp.inf, F32)
        for a in range(k):
            off, nb = _CAND_OFFS[a], _CAND_COLS[a]
            cand_ref[off:off + nb, :] = v1[a:a + 1, :] + v2[0:nb, :]
        left = _topk_rows(cand_ref[...], k)
        sel = jnp.where(left == -jnp.inf, 1.0, 0.0)
        e1 = jnp.exp(v1 - v1[0:1, :])
        e2 = jnp.exp(v2 - v2[0:1, :])
        counts, z = [], jnp.zeros((1, t), F32)
        for a in range(k):
            off, nb = _CAND_OFFS[a], _CAND_COLS[a]
            sel_a = sel[off:off + nb, :]
            counts.append(jnp.sum(sel_a, axis=0, keepdims=True))
            z = z + e1[a:a + 1, :] * jnp.sum(sel_a * e2[0:nb, :], axis=0, keepdims=True)
        i1, i2 = i1_ref[...], i2_ref[...]
        an = jnp.zeros((PEER_KEYS, t), F32)
        nn = jnp.zeros((PEER_KEYS, t), F32)
        bn = jnp.zeros((PEER_KEYS, t), F32)
        rk = jnp.full((PEER_KEYS, t), float(k), F32)
        for a in range(k):
            hit1 = keyrow == i1[a:a + 1, :]
            an = jnp.where(hit1, e1[a:a + 1, :] / z, an)
            nn = jnp.where(hit1, counts[a], nn)
            hit2 = keyrow == i2[a:a + 1, :]
            bn = jnp.where(hit2, e2[a:a + 1, :], bn)
            rk = jnp.where(hit2, float(a), rk)
        an_ref[h] = an
        nn_ref[h] = nn
        bn_ref[h] = bn.astype(BF16)
        rk_ref[h] = rk.astype(BF16)


def _route(hn2, wq_t, sk, tile):
    n = hn2.shape[0]
    k = PEER_TOPK
    tab = pl.BlockSpec((PEER_HEADS, PEER_KEYS, tile), lambda i: (0, 0, i))
    tab_shape = lambda dt: jax.ShapeDtypeStruct((PEER_HEADS, PEER_KEYS, n), dt)
    return pl.pallas_call(
        _route_body,
        grid=(n // tile,),
        in_specs=[pl.BlockSpec((tile, D_MODEL), lambda i: (i, 0)),
                  pl.BlockSpec(wq_t.shape, lambda i: (0, 0)),
                  pl.BlockSpec(sk.shape, lambda i: (0, 0, 0))],
        out_specs=[tab] * 4,
        out_shape=[tab_shape(F32), tab_shape(F32), tab_shape(BF16), tab_shape(BF16)],
        scratch_shapes=[pltpu.VMEM((k, tile), F32)] * 4 + [pltpu.VMEM((_CAND_ROWS, tile), F32)],
        compiler_params=_params(("parallel",)),
        name="route",
    )(hn2, wq_t, sk)


def _peer_body(hn_ref, u_ref, vt_ref, an_ref, nn_ref, bn_ref, rk_ref, h2_ref, gf_ref, o_ref, acc_ref,
               *, rows):
    e = pl.program_id(1)

    @pl.when(e == 0)
    def _():
        acc_ref[...] = jnp.zeros_like(acc_ref)

    act_t = lax.dot_general(u_ref[...], hn_ref[...], NT_DIMS, preferred_element_type=F32)
    pack = BF16_SUBLANES
    groups = PEER_KEYS // pack
    tt = act_t.shape[1]
    n_chunks = tt // V7X_LANES

    def packed_row(ref, h, rr):
        row = ref[h, pl.ds(e * rows + rr, 1), :]
        return jnp.broadcast_to(row, (pack, tt)).astype(BF16)

    a16 = [[packed_row(an_ref, h, rr) for rr in range(rows)] for h in range(PEER_HEADS)]
    n16 = [[packed_row(nn_ref, h, rr) for rr in range(rows)] for h in range(PEER_HEADS)]
    gates = [[None] * n_chunks for _ in range(rows)]
    for c in range(n_chunks):
        lanes = slice(c * V7X_LANES, (c + 1) * V7X_LANES)
        for h in range(PEER_HEADS):
            rk3 = rk_ref[h, :, lanes].reshape(groups, pack, V7X_LANES)
            bn3 = bn_ref[h, :, lanes].reshape(groups, pack, V7X_LANES)
            for rr in range(rows):
                keep = rk3 < n16[h][rr][None, :, lanes]
                term = a16[h][rr][None, :, lanes] * jnp.where(keep, bn3, jnp.zeros_like(bn3))
                gates[rr][c] = term if gates[rr][c] is None else gates[rr][c] + term
    pieces = []
    for rr in range(rows):
        gate = jnp.concatenate([g.reshape(PEER_KEYS, V7X_LANES) for g in gates[rr]], axis=1)
        act = act_t[rr * PEER_KEYS:(rr + 1) * PEER_KEYS, :]
        pieces.append(jax.nn.gelu(act).astype(BF16) * gate)
    acc_ref[...] += _dot(vt_ref[...], jnp.concatenate(pieces, axis=0))

    @pl.when(e == pl.num_programs(1) - 1)
    def _():
        h3 = h2_ref[...] + acc_ref[...].T
        o_ref[...] = _rms(h3, gf_ref[...])


def _peer(hn2, pu, pvt, tables, h2, g_final, tile, rows):
    n = hn2.shape[0]
    te = rows * PEER_KEYS
    n_exp = pu.shape[0]
    tok = pl.BlockSpec((tile, D_MODEL), lambda i, e: (i, 0))
    tab = pl.BlockSpec((PEER_HEADS, PEER_KEYS, tile), lambda i, e: (0, 0, i))
    return pl.pallas_call(
        functools.partial(_peer_body, rows=rows),
        grid=(n // tile, n_exp // te),
        in_specs=[tok,
                  pl.BlockSpec((te, D_MODEL), lambda i, e: (e, 0)),
                  pl.BlockSpec((D_MODEL, te), lambda i, e: (0, e)),
                  tab, tab, tab, tab, tok,
                  pl.BlockSpec(g_final.shape, lambda i, e: (0, 0))],
        out_specs=tok,
        out_shape=jax.ShapeDtypeStruct((n, D_MODEL), F32),
        scratch_shapes=[pltpu.VMEM((D_MODEL, tile), F32)],
        compiler_params=_params(("parallel", "arbitrary")),
        name="peer",
    )(hn2, pu, pvt, *tables, h2, g_final)


TOKEN_TILE = 512
ATTN_BLOCK = 256
ATTN_HEADS_PER_STEP = ATTN_BLOCK // HEAD_DIM
PEER_ROWS = 4


def kernel(x, mem, g_mix, w_in, a_re, a_im, log_dt, b_re, b_im, c_re, c_im, d_skip, w_glu, b_glu,
           g_attn_out, g_ssm_out, w_out, g_xattn, g_mem, w_xq, w_xkv, w_xo, g_ffn, w_pq, sub_keys,
           peer_u, peer_v, g_final):
    b, s, d = x.shape
    n = b * s
    depth = g_mix.shape[0]
    row = lambda v: v.reshape(1, -1).astype(F32)
    h = x
    for layer in range(depth):
        q, k, v, u = _in_proj(h.reshape(n, d), row(g_mix[layer]), w_in[layer].astype(BF16), TOKEN_TILE)
        attn = _attention(q.reshape(b, s, -1), k.reshape(b, s, -1), v.reshape(b, s, -1), ATTN_BLOCK,
                          ATTN_HEADS_PER_STEP)
        tables = _ssm_tables(a_re[layer], a_im[layer], log_dt[layer], b_re[layer], b_im[layer],
                             c_re[layer], c_im[layer], d_skip[layer], s // SSM_CHUNK)
        yssm = _ssm(u.reshape(b, s, -1), tables)
        kv = _mem_kv(mem.reshape(-1, d), row(g_mem[layer]), w_xkv[layer].astype(BF16))
        h2, hn = _mix(h, attn, yssm, kv.reshape(b, -1, 2 * d), row(g_attn_out[layer]), row(g_ssm_out[layer]),
                      w_glu[layer].astype(BF16), row(b_glu[layer]), w_out[layer].astype(BF16),
                      row(g_xattn[layer]), w_xq[layer].astype(BF16), w_xo[layer].astype(BF16),
                      row(g_ffn[layer]), TOKEN_TILE)
        hn2 = hn.reshape(n, d)
        sk = sub_keys[layer].reshape(2 * PEER_HEADS, PEER_KEYS, PEER_KEY_DIM).astype(BF16)
        routing = _route(hn2, w_pq[layer].T.astype(BF16), sk, TOKEN_TILE)
        assert depth == 1
        out = _peer(hn2, peer_u[layer].astype(BF16), peer_v[layer].T.astype(BF16), routing,
                    h2.reshape(n, d), row(g_final), TOKEN_TILE, PEER_ROWS)
        h = out.reshape(b, s, d)
    return h
```

```python
import functools
import math

import jax
import jax.numpy as jnp
from jax import lax
from jax.experimental import pallas as pl
from jax.experimental.pallas import tpu as pltpu

F32 = jnp.float32
BF16 = jnp.bfloat16

EPS = 1e-6
D_MODEL = 1024
ATTN_HEADS = 8
HEAD_DIM = 64
ATTN_WIDTH = ATTN_HEADS * HEAD_DIM
SSM_WIDTH = D_MODEL - ATTN_WIDTH
SSM_GROUP = 16
SSM_GROUPS = SSM_WIDTH // SSM_GROUP
SSM_STATE = 64
SSM_CHUNK = 16
SSM_PAIRS = SSM_GROUPS // 2
XATTN_HEADS = 4
XATTN_HEAD_DIM = D_MODEL // XATTN_HEADS
PEER_HEADS = 8
PEER_KEYS = 128
PEER_TOPK = 16
PEER_KEY_DIM = 128

V7X_LANES = 128
BF16_SUBLANES = 16
V7X_VMEM_LIMIT_BYTES = 48 * 1024 * 1024

NT_DIMS = (((1,), (1,)), ((), ()))
LOG2_E = math.log2(math.e)


def _params(semantics):
    return pltpu.CompilerParams(dimension_semantics=semantics,
                                vmem_limit_bytes=V7X_VMEM_LIMIT_BYTES)


def _rms(x, g):
    ms = jnp.mean(x * x, axis=-1, keepdims=True)
    return x * lax.rsqrt(ms + EPS) * g


def _dot(a, b):
    return jnp.dot(a, b, preferred_element_type=F32)


def _split(a):
    hi = a.astype(BF16)
    lo = (a - hi.astype(F32)).astype(BF16)
    return hi, lo


def _dot3(a, b):
    ah, al = _split(a)
    bh, bl = _split(b)
    return _dot(ah, bh) + _dot(al, bh) + _dot(ah, bl)


def _inproj_body(x_ref, g_ref, w_ref, q_ref, k_ref, v_ref, u_ref):
    xn = _rms(x_ref[...], g_ref[...]).astype(BF16)
    proj = _dot(xn, w_ref[...])
    aw = ATTN_WIDTH
    q_ref[...] = (proj[:, 0:aw] * (HEAD_DIM ** -0.5)).astype(BF16)
    k_ref[...] = proj[:, aw:2 * aw].astype(BF16)
    v_ref[...] = proj[:, 2 * aw:3 * aw].astype(BF16)
    u_ref[...] = proj[:, 3 * aw:]


def _in_proj(x2, g_mix, w_in, tile):
    n = x2.shape[0]
    tok = lambda w: pl.BlockSpec((tile, w), lambda i: (i, 0))
    full = lambda a: pl.BlockSpec(a.shape, lambda i: (0, 0))
    return pl.pallas_call(
        _inproj_body,
        grid=(n // tile,),
        in_specs=[tok(D_MODEL), full(g_mix), full(w_in)],
        out_specs=[tok(ATTN_WIDTH)] * 3 + [tok(SSM_WIDTH)],
        out_shape=[jax.ShapeDtypeStruct((n, ATTN_WIDTH), BF16)] * 3
        + [jax.ShapeDtypeStruct((n, SSM_WIDTH), F32)],
        compiler_params=_params(("parallel",)),
        name="in_proj",
    )(x2, g_mix, w_in)


def _attn_body(q_ref, k_ref, v_ref, o_ref, acc_ref, car_ref, z_ref, w_ref, *, blk, heads):
    i = pl.program_id(2)
    width = heads * HEAD_DIM
    qa = q_ref[0]
    lane = lax.broadcasted_iota(jnp.int32, (blk, width), 1)
    in_head = [(lane >= h * HEAD_DIM) & (lane < (h + 1) * HEAD_DIM) for h in range(heads)]
    zero = jnp.zeros_like(qa)
    q_heads = [jnp.where(m, qa, zero) for m in in_head]
    row = lax.broadcasted_iota(jnp.int32, (blk, blk), 0)
    col = lax.broadcasted_iota(jnp.int32, (blk, blk), 1)
    tri = jnp.where(row > col, 1.0, 0.0).astype(BF16)
    causal = col < row

    acc_ref[...] = jnp.zeros_like(acc_ref)
    car_ref[...] = jnp.zeros_like(car_ref)

    def scores(j):
        kb = k_ref[0, pl.ds(pl.multiple_of(j * blk, blk), blk), :]
        for h in range(heads):
            z_ref[h] = lax.dot_general(q_heads[h], kb, NT_DIMS, preferred_element_type=F32)

    def weights(mask):
        for h in range(heads):
            z = z_ref[h]
            sp = jnp.maximum(z, 0.0) + jnp.log(1.0 + jnp.exp2(jnp.abs(z) * -LOG2_E))
            if mask is not None:
                sp = jnp.where(mask, sp, 0.0)
            inner = _dot(sp.astype(BF16), tri)
            carry = car_ref[h]
            later = inner + jnp.concatenate([carry] * (blk // V7X_LANES), axis=1)
            w = jnp.exp2((z - sp - later) * LOG2_E)
            if mask is not None:
                w = jnp.where(mask, w, 0.0)
            w_ref[:, h * blk:(h + 1) * blk] = w.astype(BF16)
            total = inner[:, 0:1] + sp[:, 0:1]
            car_ref[h] = carry + jnp.broadcast_to(total, carry.shape)

    def values(j):
        vb = v_ref[0, pl.ds(pl.multiple_of(j * blk, blk), blk), :]
        vs = [jnp.where(in_head[h], vb, jnp.zeros_like(vb)) for h in range(heads)]
        acc_ref[...] += _dot(w_ref[...], jnp.concatenate(vs, axis=0))

    scores(i)
    weights(causal)
    scores(jnp.maximum(i - 1, 0))

    def body(t, _):
        j = i - 1 - t
        values(j + 1)
        weights(None)
        scores(jnp.maximum(j - 1, 0))
        return 0

    lax.fori_loop(0, i, body, 0)
    values(0)
    o_ref[0] = acc_ref[...]


def _attention(q, k, v, blk, heads):
    b, s, _ = q.shape
    width = heads * HEAD_DIM
    assert blk == width, "the value masks reuse the (blk, width) lane iota"
    qspec = pl.BlockSpec((1, blk, width), lambda bi, p, i: (bi, i, p))
    kvspec = pl.BlockSpec((1, s, width), lambda bi, p, i: (bi, 0, p))
    return pl.pallas_call(
        functools.partial(_attn_body, blk=blk, heads=heads),
        grid=(b, ATTN_WIDTH // width, s // blk),
        in_specs=[qspec, kvspec, kvspec],
        out_specs=qspec,
        out_shape=jax.ShapeDtypeStruct((b, s, ATTN_WIDTH), F32),
        scratch_shapes=[pltpu.VMEM((blk, width), F32),
                        pltpu.VMEM((heads, blk, V7X_LANES), F32),
                        pltpu.VMEM((heads, blk, blk), F32),
                        pltpu.VMEM((blk, heads * blk), BF16)],
        compiler_params=_params(("parallel", "parallel", "arbitrary")),
        name="attn",
    )(q, k, v)


def _ssm_body(x_ref, dk_ref, ninj_ref, pout_ref, lam_ref, d_ref, y_ref, m_ref, xh_ref, xl_ref, car_ref,
              *, rc, tiles_per_batch):
    r = pl.program_id(1)
    L, W = SSM_CHUNK, V7X_LANES
    half = (W // SSM_GROUP) * SSM_STATE

    @pl.when(r == 0)
    def _():
        m_ref[...] = jnp.zeros_like(m_ref)
        for lag in range(L):
            dk = dk_ref[0, lag].astype(BF16)
            for s in range(L - lag):
                t = s + lag
                m_ref[s * W:(s + 1) * W, t * W:(t + 1) * W] = dk

    @pl.when(r % tiles_per_batch == 0)
    def _():
        car_ref[...] = jnp.zeros_like(car_ref)

    for s in range(L):
        hi, lo = _split(x_ref[pl.ds(s, rc, stride=L), :])
        xh_ref[:, s * W:(s + 1) * W] = hi
        xl_ref[:, s * W:(s + 1) * W] = lo

    def dot2(ah, al, w):
        return _dot(ah, w) + _dot(al, w)

    s_all = dot2(xh_ref[...], xl_ref[...], ninj_ref[0])
    sr, si = s_all[:, :half], s_all[:, half:]
    row = lax.broadcasted_iota(jnp.int32, (rc, half), 0)
    cr, ci = car_ref[0:1, :half], car_ref[0:1, half:]
    ar, ai = lam_ref[0, 0:1, :half], lam_ref[0, 0:1, half:]
    first = row == 0
    sr = sr + jnp.where(first, ar * cr - ai * ci, 0.0)
    si = si + jnp.where(first, ar * ci + ai * cr, 0.0)
    d, lvl = 1, 0
    while d < rc:
        ar, ai = lam_ref[0, lvl:lvl + 1, :half], lam_ref[0, lvl:lvl + 1, half:]
        keep = row >= d
        pr = jnp.where(keep, pltpu.roll(sr, d, 0), 0.0)
        pi = jnp.where(keep, pltpu.roll(si, d, 0), 0.0)
        sr, si = sr + ar * pr - ai * pi, si + ar * pi + ai * pr
        d, lvl = 2 * d, lvl + 1
    prev = jnp.concatenate([jnp.where(first, cr, pltpu.roll(sr, 1, 0)),
                            jnp.where(first, ci, pltpu.roll(si, 1, 0))], axis=1)
    car_ref[0:1, :half] = sr[rc - 1:rc, :]
    car_ref[0:1, half:] = si[rc - 1:rc, :]
    ph, plo = _split(prev)
    far = dot2(ph, plo, pout_ref[0])
    for jt in range(L // 2):
        kk = (2 * jt + 2) * W
        cols = slice(2 * jt * W, (2 * jt + 2) * W)
        yt = dot2(xh_ref[:, :kk], xl_ref[:, :kk], m_ref[:kk, cols]) + far[:, cols]
        for t in (2 * jt, 2 * jt + 1):
            yv = yt[:, (t - 2 * jt) * W:(t - 2 * jt + 1) * W] + d_ref[0] * x_ref[pl.ds(t, rc, stride=L), :]
            y_ref[pl.ds(t, rc, stride=L), :] = jax.nn.gelu(yv)


def _ssm_tables(a_re, a_im, log_dt, b_re, b_im, c_re, c_im, d_skip, rc):
    L, G, P, C = SSM_CHUNK, SSM_GROUPS, SSM_STATE, SSM_GROUP
    gb = V7X_LANES // C
    nblk = G // gb
    ar, ai = a_re.astype(F32), a_im.astype(F32)
    dt = jnp.exp(log_dt.astype(F32))[:, None]
    ldr, ldi = ar * dt, ai * dt

    def lam_pow(k):
        mag = jnp.exp(k * ldr)
        return mag * jnp.cos(k * ldi), mag * jnp.sin(k * ldi)

    lbr, lbi = lam_pow(1.0)
    den = ar * ar + ai * ai
    qr = ((lbr - 1.0) * ar + lbi * ai) / den
    qi = (lbi * ar - (lbr - 1.0) * ai) / den
    br, bi = b_re.astype(F32), b_im.astype(F32)
    bbr = qr[..., None] * br - qi[..., None] * bi
    bbi = qr[..., None] * bi + qi[..., None] * br
    cre, cim = c_re.astype(F32), c_im.astype(F32)
    steps = jnp.arange(L + 1, dtype=F32)[:, None, None]
    pwr, pwi = lam_pow(steps)
    wr = pwr[:L, :, :, None] * bbr[None] - pwi[:L, :, :, None] * bbi[None]
    wi = pwr[:L, :, :, None] * bbi[None] + pwi[:L, :, :, None] * bbr[None]
    kern = jnp.einsum('gcp,kgpd->kgcd', cre, wr) - jnp.einsum('gcp,kgpd->kgcd', cim, wi)
    eye = jnp.eye(gb, dtype=F32)
    dk = jnp.einsum('kjgcd,gh->jkgdhc', kern.reshape(L, nblk, gb, C, C), eye).reshape(nblk, L, V7X_LANES, V7X_LANES)
    inj = lambda w: jnp.einsum('sjgpd,gh->jsgdhp', w[::-1].reshape(L, nblk, gb, P, C), eye).reshape(
        nblk, L * V7X_LANES, gb * P)
    ninj = jnp.concatenate([inj(wr), inj(wi)], axis=2)
    o_re = cre[None] * pwr[1:, :, None, :] - cim[None] * pwi[1:, :, None, :]
    o_im = cre[None] * pwi[1:, :, None, :] + cim[None] * pwr[1:, :, None, :]
    out = lambda w: jnp.einsum('tjgcp,gh->jgpthc', w.reshape(L, nblk, gb, C, P), eye).reshape(
        nblk, gb * P, L * V7X_LANES)
    pout = jnp.concatenate([out(o_re), out(-o_im)], axis=1)
    levels = []
    d = 1
    while d < max(rc, 2):
        lr, li = lam_pow(float(L * d))
        levels.append(jnp.concatenate([lr.reshape(nblk, gb * P), li.reshape(nblk, gb * P)], axis=1))
        d *= 2
    lam_tab = jnp.stack(levels, axis=1)
    dvec = d_skip.astype(F32).reshape(nblk, 1, V7X_LANES)
    return dk, ninj.astype(BF16), pout.astype(BF16), lam_tab, dvec


def _ssm(u2, tables, seq, tile):
    n = u2.shape[0]
    L, W = SSM_CHUNK, V7X_LANES
    rc = tile // L
    dk, ninj, pout, lam_tab, dvec = tables
    nblk = dk.shape[0]
    per_blk = lambda a: pl.BlockSpec((1,) + a.shape[1:], lambda j, r: (j,) + (0,) * (a.ndim - 1))
    tok = pl.BlockSpec((tile, W), lambda j, r: (r, j))
    return pl.pallas_call(
        functools.partial(_ssm_body, rc=rc, tiles_per_batch=seq // tile),
        grid=(nblk, n // tile),
        in_specs=[tok] + [per_blk(a) for a in (dk, ninj, pout, lam_tab, dvec)],
        out_specs=tok,
        out_shape=jax.ShapeDtypeStruct(u2.shape, F32),
        scratch_shapes=[pltpu.VMEM((L * W, L * W), BF16),
                        pltpu.VMEM((rc, L * W), BF16),
                        pltpu.VMEM((rc, L * W), BF16),
                        pltpu.VMEM((8, ninj.shape[2]), F32)],
        compiler_params=_params(("parallel", "arbitrary")),
        name="ssm",
    )(u2, dk, ninj, pout, lam_tab, dvec)


def _memkv_body(m_ref, g_ref, w_ref, kv_ref):
    mn = _rms(m_ref[...], g_ref[...]).astype(BF16)
    kv_ref[...] = _dot(mn, w_ref[...]).astype(BF16)


def _mem_kv(mem2, g_mem, w_xkv):
    full = lambda a: pl.BlockSpec(a.shape, lambda i: (0, 0))
    return pl.pallas_call(
        _memkv_body,
        grid=(1,),
        in_specs=[full(mem2), full(g_mem), full(w_xkv)],
        out_specs=pl.BlockSpec((mem2.shape[0], 2 * D_MODEL), lambda i: (0, 0)),
        out_shape=jax.ShapeDtypeStruct((mem2.shape[0], 2 * D_MODEL), BF16),
        compiler_params=_params(("arbitrary",)),
        name="mem_kv",
    )(mem2, g_mem, w_xkv)


def _mix_body(x_ref, a_ref, y_ref, gao_ref, gso_ref, wglu_ref, bglu_ref, wout_ref, gx_ref,
              wxq_ref, kv_ref, wxo_ref, gffn_ref, h2_ref, hn_ref):
    an = _rms(a_ref[0], gao_ref[...]).astype(BF16)
    y = y_ref[0]
    gate = 1.0 / (1.0 + jnp.exp(-(_dot(y.astype(BF16), wglu_ref[...]) + bglu_ref[...])))
    sn = _rms(y * gate, gso_ref[...]).astype(BF16)
    h1 = x_ref[0] + _dot(jnp.concatenate([an, sn], axis=1), wout_ref[...])
    qx = _dot(_rms(h1, gx_ref[...]).astype(BF16), wxq_ref[...])
    heads = []
    for h in range(XATTN_HEADS):
        lo, hi = h * XATTN_HEAD_DIM, (h + 1) * XATTN_HEAD_DIM
        km = kv_ref[0, :, lo:hi]
        vm = kv_ref[0, :, D_MODEL + lo:D_MODEL + hi]
        sc = lax.dot_general(qx[:, lo:hi].astype(BF16), km, NT_DIMS,
                             preferred_element_type=F32) * (XATTN_HEAD_DIM ** -0.5)
        e = jnp.exp(sc - jnp.max(sc, axis=-1, keepdims=True))
        p = e / jnp.sum(e, axis=-1, keepdims=True)
        heads.append(_dot(p.astype(BF16), vm).astype(BF16))
    h2 = h1 + _dot(jnp.concatenate(heads, axis=1), wxo_ref[...])
    h2_ref[0] = h2
    hn_ref[0] = _rms(h2, gffn_ref[...]).astype(BF16)


def _mix(x, attn, yssm, kv, g_attn_out, g_ssm_out, w_glu, b_glu, w_out, g_xattn, w_xq, w_xo, g_ffn, tile):
    b, s, _ = x.shape
    tok = lambda w: pl.BlockSpec((1, tile, w), lambda bi, i: (bi, i, 0))
    full = lambda a: pl.BlockSpec(a.shape, lambda bi, i: (0,) * a.ndim)
    kvspec = pl.BlockSpec((1,) + kv.shape[1:], lambda bi, i: (bi, 0, 0))
    return pl.pallas_call(
        _mix_body,
        grid=(b, s // tile),
        in_specs=[tok(D_MODEL), tok(ATTN_WIDTH), tok(SSM_WIDTH), full(g_attn_out), full(g_ssm_out),
                  full(w_glu), full(b_glu), full(w_out), full(g_xattn), full(w_xq), kvspec,
                  full(w_xo), full(g_ffn)],
        out_specs=[tok(D_MODEL), tok(D_MODEL)],
        out_shape=[jax.ShapeDtypeStruct((b, s, D_MODEL), F32),
                   jax.ShapeDtypeStruct((b, s, D_MODEL), BF16)],
        compiler_params=_params(("parallel", "parallel")),
        name="mix",
    )(x, attn, yssm, g_attn_out, g_ssm_out, w_glu, b_glu, w_out, g_xattn, w_xq, kv, w_xo, g_ffn)


def _topk_rows(s, k, val_ref=None, idx_ref=None):
    n = s.shape[0]
    rowid = lax.broadcasted_iota(jnp.int32, s.shape, 0).astype(F32)
    for r in range(k):
        m = jnp.max(s, axis=0, keepdims=True)
        ix = jnp.min(jnp.where(s == m, rowid, float(n)), axis=0, keepdims=True)
        if val_ref is not None:
            val_ref[r:r + 1, :] = m
            idx_ref[r:r + 1, :] = ix
        s = jnp.where(rowid == ix, -jnp.inf, s)
    return s


_CAND_COLS = [PEER_TOPK // (a + 1) for a in range(PEER_TOPK)]
_CAND_OFFS = [sum(_CAND_COLS[:a]) for a in range(PEER_TOPK)]
_CAND_ROWS = -(-sum(_CAND_COLS) // 8) * 8


def _route_body(hn_ref, wq_ref, sk_ref, an_ref, nn_ref, bn_ref, rk_ref,
                v1_ref, i1_ref, v2_ref, i2_ref, cand_ref):
    k = PEER_TOPK
    qt = lax.dot_general(wq_ref[...], hn_ref[...], NT_DIMS, preferred_element_type=F32)
    t = qt.shape[1]
    keyrow = lax.broadcasted_iota(jnp.int32, (PEER_KEYS, t), 0).astype(F32)
    for h in range(PEER_HEADS):
        for half, (vr, ir) in enumerate(((v1_ref, i1_ref), (v2_ref, i2_ref))):
            j = 2 * h + half
            qh = qt[j * PEER_KEY_DIM:(j + 1) * PEER_KEY_DIM, :].astype(BF16)
            _topk_rows(_dot(sk_ref[j], qh), k, vr, ir)
        v1, v2 = v1_ref[...], v2_ref[...]
        cand_ref[...] = jnp.full(cand_ref.shape, -jnp.inf, F32)
        for a in range(k):
            off, nb = _CAND_OFFS[a], _CAND_COLS[a]
            cand_ref[off:off + nb, :] = v1[a:a + 1, :] + v2[0:nb, :]
        left = _topk_rows(cand_ref[...], k)
        sel = jnp.where(left == -jnp.inf, 1.0, 0.0)
        e1 = jnp.exp(v1 - v1[0:1, :])
        e2 = jnp.exp(v2 - v2[0:1, :])
        counts, z = [], jnp.zeros((1, t), F32)
        for a in range(k):
            off, nb = _CAND_OFFS[a], _CAND_COLS[a]
            sel_a = sel[off:off + nb, :]
            counts.append(jnp.sum(sel_a, axis=0, keepdims=True))
            z = z + e1[a:a + 1, :] * jnp.sum(sel_a * e2[0:nb, :], axis=0, keepdims=True)
        i1, i2 = i1_ref[...], i2_ref[...]
        an = jnp.zeros((PEER_KEYS, t), F32)
        nn = jnp.zeros((PEER_KEYS, t), F32)
        bn = jnp.zeros((PEER_KEYS, t), F32)
        rk = jnp.full((PEER_KEYS, t), float(k), F32)
        for a in range(k):
            hit1 = keyrow == i1[a:a + 1, :]
            an = jnp.where(hit1, e1[a:a + 1, :] / z, an)
            nn = jnp.where(hit1, counts[a], nn)
            hit2 = keyrow == i2[a:a + 1, :]
            bn = jnp.where(hit2, e2[a:a + 1, :], bn)
            rk = jnp.where(hit2, float(a), rk)
        an_ref[h] = an
        nn_ref[h] = nn
        bn_ref[h] = bn.astype(BF16)
        rk_ref[h] = rk.astype(BF16)


def _route(hn2, wq_t, sk, tile):
    n = hn2.shape[0]
    k = PEER_TOPK
    tab = pl.BlockSpec((PEER_HEADS, PEER_KEYS, tile), lambda i: (0, 0, i))
    tab_shape = lambda dt: jax.ShapeDtypeStruct((PEER_HEADS, PEER_KEYS, n), dt)
    return pl.pallas_call(
        _route_body,
        grid=(n // tile,),
        in_specs=[pl.BlockSpec((tile, D_MODEL), lambda i: (i, 0)),
                  pl.BlockSpec(wq_t.shape, lambda i: (0, 0)),
                  pl.BlockSpec(sk.shape, lambda i: (0, 0, 0))],
        out_specs=[tab] * 4,
        out_shape=[tab_shape(F32), tab_shape(F32), tab_shape(BF16), tab_shape(BF16)],
        scratch_shapes=[pltpu.VMEM((k, tile), F32)] * 4 + [pltpu.VMEM((_CAND_ROWS, tile), F32)],
        compiler_params=_params(("parallel",)),
        name="route",
    )(hn2, wq_t, sk)


def _peer_body(hn_ref, u_ref, vt_ref, an_ref, nn_ref, bn_ref, rk_ref, h2_ref, gf_ref, o_ref,
               acc_ref, w_ref, act_ref, rows_ref, *, rows):
    e = pl.program_id(1)

    @pl.when(e == 0)
    def _():
        acc_ref[...] = jnp.zeros_like(acc_ref)

    act_ref[...] = lax.dot_general(u_ref[...], hn_ref[...], NT_DIMS, preferred_element_type=F32)
    for h in range(PEER_HEADS):
        for rr in range(rows):
            pos = 2 * (h * rows + rr)
            rows_ref[pos:pos + 1, :] = an_ref[h, pl.ds(e * rows + rr, 1), :]
            rows_ref[pos + 1:pos + 2, :] = nn_ref[h, pl.ds(e * rows + rr, 1), :]
    pack = BF16_SUBLANES
    groups = PEER_KEYS // pack
    for rr in range(rows):
        keys = slice(rr * PEER_KEYS, (rr + 1) * PEER_KEYS)
        for c in range(act_ref.shape[1] // V7X_LANES):
            lanes = slice(c * V7X_LANES, (c + 1) * V7X_LANES)
            gate = None
            for h in range(PEER_HEADS):
                pos = 2 * (h * rows + rr)
                a16 = jnp.broadcast_to(rows_ref[pos:pos + 1, lanes], (pack, V7X_LANES)).astype(BF16)
                n16 = jnp.broadcast_to(rows_ref[pos + 1:pos + 2, lanes], (pack, V7X_LANES)).astype(BF16)
                rk3 = rk_ref[h, :, lanes].reshape(groups, pack, V7X_LANES)
                bn3 = bn_ref[h, :, lanes].reshape(groups, pack, V7X_LANES)
                term = a16[None] * jnp.where(rk3 < n16[None], bn3, jnp.zeros_like(bn3))
                gate = term if gate is None else gate + term
            act = act_ref[keys, lanes]
            w_ref[keys, lanes] = jax.nn.gelu(act).astype(BF16) * gate.reshape(PEER_KEYS, V7X_LANES)
    acc_ref[...] += _dot(vt_ref[...], w_ref[...])

    @pl.when(e == pl.num_programs(1) - 1)
    def _():
        h3 = h2_ref[...] + acc_ref[...].T
        o_ref[...] = _rms(h3, gf_ref[...])


def _peer(hn2, pu, pvt, tables, h2, g_final, tile, rows):
    n = hn2.shape[0]
    te = rows * PEER_KEYS
    n_tiles = pu.shape[0] // te
    tok = pl.BlockSpec((tile, D_MODEL), lambda i, e: (i, 0))
    tab = pl.BlockSpec((PEER_HEADS, PEER_KEYS, tile), lambda i, e: (0, 0, i))
    return pl.pallas_call(
        functools.partial(_peer_body, rows=rows),
        grid=(n // tile, n_tiles),
        in_specs=[tok,
                  pl.BlockSpec((te, D_MODEL), lambda i, e: (e, 0)),
                  pl.BlockSpec((D_MODEL, te), lambda i, e: (0, e)),
                  tab, tab, tab, tab, tok,
                  pl.BlockSpec(g_final.shape, lambda i, e: (0, 0))],
        out_specs=tok,
        out_shape=jax.ShapeDtypeStruct((n, D_MODEL), F32),
        scratch_shapes=[pltpu.VMEM((D_MODEL, tile), F32),
                        pltpu.VMEM((te, tile), BF16),
                        pltpu.VMEM((te, tile), F32),
                        pltpu.VMEM((2 * PEER_HEADS * rows, tile), F32)],
        compiler_params=_params(("parallel", "arbitrary")),
        name="peer",
    )(hn2, pu, pvt, *tables, h2, g_final)


TOKEN_TILE = 512
ATTN_BLOCK = 256
ATTN_HEADS_PER_STEP = ATTN_BLOCK // HEAD_DIM
PEER_ROWS = 4
SSM_TILE = 4096


def kernel(x, mem, g_mix, w_in, a_re, a_im, log_dt, b_re, b_im, c_re, c_im, d_skip, w_glu, b_glu,
           g_attn_out, g_ssm_out, w_out, g_xattn, g_mem, w_xq, w_xkv, w_xo, g_ffn, w_pq, sub_keys,
           peer_u, peer_v, g_final):
    b, s, d = x.shape
    n = b * s
    depth = g_mix.shape[0]
    row = lambda v: v.reshape(1, -1).astype(F32)
    h = x
    for layer in range(depth):
        q, k, v, u = _in_proj(h.reshape(n, d), row(g_mix[layer]), w_in[layer].astype(BF16), TOKEN_TILE)
        attn = _attention(q.reshape(b, s, -1), k.reshape(b, s, -1), v.reshape(b, s, -1), ATTN_BLOCK,
                          ATTN_HEADS_PER_STEP)
        ssm_tile = min(SSM_TILE, s)
        tables = _ssm_tables(a_re[layer], a_im[layer], log_dt[layer], b_re[layer], b_im[layer],
                             c_re[layer], c_im[layer], d_skip[layer], ssm_tile // SSM_CHUNK)
        yssm = _ssm(u, tables, s, ssm_tile).reshape(b, s, -1)
        kv = _mem_kv(mem.reshape(-1, d), row(g_mem[layer]), w_xkv[layer].astype(BF16))
        h2, hn = _mix(h, attn, yssm, kv.reshape(b, -1, 2 * d), row(g_attn_out[layer]), row(g_ssm_out[layer]),
                      w_glu[layer].astype(BF16), row(b_glu[layer]), w_out[layer].astype(BF16),
                      row(g_xattn[layer]), w_xq[layer].astype(BF16), w_xo[layer].astype(BF16),
                      row(g_ffn[layer]), TOKEN_TILE)
        hn2 = hn.reshape(n, d)
        sk = sub_keys[layer].reshape(2 * PEER_HEADS, PEER_KEYS, PEER_KEY_DIM).astype(BF16)
        routing = _route(hn2, w_pq[layer].T.astype(BF16), sk, TOKEN_TILE)
        assert depth == 1
        out = _peer(hn2, peer_u[layer].astype(BF16), peer_v[layer].T.astype(BF16), routing,
                    h2.reshape(n, d), row(g_final), TOKEN_TILE, PEER_ROWS)
        h = out.reshape(b, s, d)
    return h
```

```python
import functools
import math

import jax
import jax.numpy as jnp
from jax import lax
from jax.experimental import pallas as pl
from jax.experimental.pallas import tpu as pltpu

F32 = jnp.float32
BF16 = jnp.bfloat16

EPS = 1e-6
D_MODEL = 1024
ATTN_HEADS = 8
HEAD_DIM = 64
ATTN_WIDTH = ATTN_HEADS * HEAD_DIM
SSM_WIDTH = D_MODEL - ATTN_WIDTH
SSM_GROUP = 16
SSM_GROUPS = SSM_WIDTH // SSM_GROUP
SSM_STATE = 64
SSM_CHUNK = 16
SSM_PAIRS = SSM_GROUPS // 2
XATTN_HEADS = 4
XATTN_HEAD_DIM = D_MODEL // XATTN_HEADS
PEER_HEADS = 8
PEER_KEYS = 128
PEER_TOPK = 16
PEER_KEY_DIM = 128

V7X_LANES = 128
BF16_SUBLANES = 16
V7X_VMEM_LIMIT_BYTES = 48 * 1024 * 1024

NT_DIMS = (((1,), (1,)), ((), ()))
LOG2_E = math.log2(math.e)


def _params(semantics):
    return pltpu.CompilerParams(dimension_semantics=semantics,
                                vmem_limit_bytes=V7X_VMEM_LIMIT_BYTES)


def _rms(x, g):
    ms = jnp.mean(x * x, axis=-1, keepdims=True)
    return x * lax.rsqrt(ms + EPS) * g


def _dot(a, b):
    return jnp.dot(a, b, preferred_element_type=F32)


def _split(a):
    hi = a.astype(BF16)
    lo = (a - hi.astype(F32)).astype(BF16)
    return hi, lo


def _dot3(a, b):
    ah, al = _split(a)
    bh, bl = _split(b)
    return _dot(ah, bh) + _dot(al, bh) + _dot(ah, bl)


def _inproj_body(x_ref, g_ref, w_ref, q_ref, k_ref, v_ref, u_ref):
    xn = _rms(x_ref[...], g_ref[...]).astype(BF16)
    proj = _dot(xn, w_ref[...])
    aw = ATTN_WIDTH
    q_ref[...] = (proj[:, 0:aw] * (HEAD_DIM ** -0.5)).astype(BF16)
    k_ref[...] = proj[:, aw:2 * aw].astype(BF16)
    v_ref[...] = proj[:, 2 * aw:3 * aw].astype(BF16)
    u_ref[...] = proj[:, 3 * aw:]


def _in_proj(x2, g_mix, w_in, tile):
    n = x2.shape[0]
    tok = lambda w: pl.BlockSpec((tile, w), lambda i: (i, 0))
    full = lambda a: pl.BlockSpec(a.shape, lambda i: (0, 0))
    return pl.pallas_call(
        _inproj_body,
        grid=(n // tile,),
        in_specs=[tok(D_MODEL), full(g_mix), full(w_in)],
        out_specs=[tok(ATTN_WIDTH)] * 3 + [tok(SSM_WIDTH)],
        out_shape=[jax.ShapeDtypeStruct((n, ATTN_WIDTH), BF16)] * 3
        + [jax.ShapeDtypeStruct((n, SSM_WIDTH), F32)],
        compiler_params=_params(("parallel",)),
        name="in_proj",
    )(x2, g_mix, w_in)


def _attn_body(q_ref, k_ref, v_ref, o_ref, acc_ref, car_ref, z_ref, w_ref, *, blk, heads):
    i = pl.program_id(2)
    width = heads * HEAD_DIM
    qa = q_ref[0]
    lane = lax.broadcasted_iota(jnp.int32, (blk, width), 1)
    in_head = [(lane >= h * HEAD_DIM) & (lane < (h + 1) * HEAD_DIM) for h in range(heads)]
    zero = jnp.zeros_like(qa)
    q_heads = [jnp.where(m, qa, zero) for m in in_head]
    row = lax.broadcasted_iota(jnp.int32, (blk, blk), 0)
    col = lax.broadcasted_iota(jnp.int32, (blk, blk), 1)
    tri = jnp.where(row > col, 1.0, 0.0).astype(BF16)
    causal = col < row

    acc_ref[...] = jnp.zeros_like(acc_ref)
    car_ref[...] = jnp.zeros_like(car_ref)

    def scores(j):
        kb = k_ref[0, pl.ds(pl.multiple_of(j * blk, blk), blk), :]
        for h in range(heads):
            z_ref[h] = lax.dot_general(q_heads[h], kb, NT_DIMS, preferred_element_type=F32)

    def weights(mask):
        for h in range(heads):
            z = z_ref[h]
            sp = jnp.maximum(z, 0.0) + jnp.log(1.0 + jnp.exp2(jnp.abs(z) * -LOG2_E))
            if mask is not None:
                sp = jnp.where(mask, sp, 0.0)
            inner = _dot(sp.astype(BF16), tri)
            carry = car_ref[h]
            later = inner + jnp.concatenate([carry] * (blk // V7X_LANES), axis=1)
            w = jnp.exp2((z - sp - later) * LOG2_E)
            if mask is not None:
                w = jnp.where(mask, w, 0.0)
            w_ref[:, h * blk:(h + 1) * blk] = w.astype(BF16)
            total = inner[:, 0:1] + sp[:, 0:1]
            car_ref[h] = carry + jnp.broadcast_to(total, carry.shape)

    def values(j):
        vb = v_ref[0, pl.ds(pl.multiple_of(j * blk, blk), blk), :]
        vs = [jnp.where(in_head[h], vb, jnp.zeros_like(vb)) for h in range(heads)]
        acc_ref[...] += _dot(w_ref[...], jnp.concatenate(vs, axis=0))

    scores(i)
    weights(causal)
    scores(jnp.maximum(i - 1, 0))

    def body(t, _):
        j = i - 1 - t
        values(j + 1)
        weights(None)
        scores(jnp.maximum(j - 1, 0))
        return 0

    lax.fori_loop(0, i, body, 0)
    values(0)
    o_ref[0] = acc_ref[...]


def _attention(q, k, v, blk, heads):
    b, s, _ = q.shape
    width = heads * HEAD_DIM
    assert blk == width, "the value masks reuse the (blk, width) lane iota"
    qspec = pl.BlockSpec((1, blk, width), lambda bi, p, i: (bi, i, p))
    kvspec = pl.BlockSpec((1, s, width), lambda bi, p, i: (bi, 0, p))
    return pl.pallas_call(
        functools.partial(_attn_body, blk=blk, heads=heads),
        grid=(b, ATTN_WIDTH // width, s // blk),
        in_specs=[qspec, kvspec, kvspec],
        out_specs=qspec,
        out_shape=jax.ShapeDtypeStruct((b, s, ATTN_WIDTH), F32),
        scratch_shapes=[pltpu.VMEM((blk, width), F32),
                        pltpu.VMEM((heads, blk, V7X_LANES), F32),
                        pltpu.VMEM((heads, blk, blk), F32),
                        pltpu.VMEM((blk, heads * blk), BF16)],
        compiler_params=_params(("parallel", "parallel", "arbitrary")),
        name="attn",
    )(q, k, v)


def _ssm_body(x_ref, dk_ref, ninj_ref, pout_ref, lam_ref, d_ref, y_ref, m_ref, xh_ref, xl_ref, car_ref,
              *, rc, tiles_per_batch):
    r = pl.program_id(1)
    L, W = SSM_CHUNK, V7X_LANES
    half = (W // SSM_GROUP) * SSM_STATE

    @pl.when(r == 0)
    def _():
        m_ref[...] = jnp.zeros_like(m_ref)
        for lag in range(L):
            dk = dk_ref[0, lag].astype(BF16)
            for s in range(L - lag):
                t = s + lag
                m_ref[s * W:(s + 1) * W, t * W:(t + 1) * W] = dk

    @pl.when(r % tiles_per_batch == 0)
    def _():
        car_ref[...] = jnp.zeros_like(car_ref)

    for s in range(L):
        hi, lo = _split(x_ref[pl.ds(s, rc, stride=L), :])
        xh_ref[:, s * W:(s + 1) * W] = hi
        xl_ref[:, s * W:(s + 1) * W] = lo

    def dot2(ah, al, w):
        return _dot(ah, w) + _dot(al, w)

    s_all = dot2(xh_ref[...], xl_ref[...], ninj_ref[0])
    sr, si = s_all[:, :half], s_all[:, half:]
    row = lax.broadcasted_iota(jnp.int32, (rc, half), 0)
    cr, ci = car_ref[0:1, :half], car_ref[0:1, half:]
    ar, ai = lam_ref[0, 0:1, :half], lam_ref[0, 0:1, half:]
    first = row == 0
    sr = sr + jnp.where(first, ar * cr - ai * ci, 0.0)
    si = si + jnp.where(first, ar * ci + ai * cr, 0.0)
    d, lvl = 1, 0
    while d < rc:
        ar, ai = lam_ref[0, lvl:lvl + 1, :half], lam_ref[0, lvl:lvl + 1, half:]
        keep = row >= d
        pr = jnp.where(keep, pltpu.roll(sr, d, 0), 0.0)
        pi = jnp.where(keep, pltpu.roll(si, d, 0), 0.0)
        sr, si = sr + ar * pr - ai * pi, si + ar * pi + ai * pr
        d, lvl = 2 * d, lvl + 1
    prev = jnp.concatenate([jnp.where(first, cr, pltpu.roll(sr, 1, 0)),
                            jnp.where(first, ci, pltpu.roll(si, 1, 0))], axis=1)
    car_ref[0:1, :half] = sr[rc - 1:rc, :]
    car_ref[0:1, half:] = si[rc - 1:rc, :]
    ph, plo = _split(prev)
    far = dot2(ph, plo, pout_ref[0])
    for jt in range(L // 2):
        kk = (2 * jt + 2) * W
        cols = slice(2 * jt * W, (2 * jt + 2) * W)
        yt = dot2(xh_ref[:, :kk], xl_ref[:, :kk], m_ref[:kk, cols]) + far[:, cols]
        for t in (2 * jt, 2 * jt + 1):
            yv = yt[:, (t - 2 * jt) * W:(t - 2 * jt + 1) * W] + d_ref[0] * x_ref[pl.ds(t, rc, stride=L), :]
            y_ref[pl.ds(t, rc, stride=L), :] = jax.nn.gelu(yv)


def _ssm_tables(a_re, a_im, log_dt, b_re, b_im, c_re, c_im, d_skip, rc):
    L, G, P, C = SSM_CHUNK, SSM_GROUPS, SSM_STATE, SSM_GROUP
    gb = V7X_LANES // C
    nblk = G // gb
    ar, ai = a_re.astype(F32), a_im.astype(F32)
    dt = jnp.exp(log_dt.astype(F32))[:, None]
    ldr, ldi = ar * dt, ai * dt

    def lam_pow(k):
        mag = jnp.exp(k * ldr)
        return mag * jnp.cos(k * ldi), mag * jnp.sin(k * ldi)

    lbr, lbi = lam_pow(1.0)
    den = ar * ar + ai * ai
    qr = ((lbr - 1.0) * ar + lbi * ai) / den
    qi = (lbi * ar - (lbr - 1.0) * ai) / den
    br, bi = b_re.astype(F32), b_im.astype(F32)
    bbr = qr[..., None] * br - qi[..., None] * bi
    bbi = qr[..., None] * bi + qi[..., None] * br
    cre, cim = c_re.astype(F32), c_im.astype(F32)
    steps = jnp.arange(L + 1, dtype=F32)[:, None, None]
    pwr, pwi = lam_pow(steps)
    wr = pwr[:L, :, :, None] * bbr[None] - pwi[:L, :, :, None] * bbi[None]
    wi = pwr[:L, :, :, None] * bbi[None] + pwi[:L, :, :, None] * bbr[None]
    kern = jnp.einsum('gcp,kgpd->kgcd', cre, wr) - jnp.einsum('gcp,kgpd->kgcd', cim, wi)
    eye = jnp.eye(gb, dtype=F32)
    dk = jnp.einsum('kjgcd,gh->jkgdhc', kern.reshape(L, nblk, gb, C, C), eye).reshape(nblk, L, V7X_LANES, V7X_LANES)
    inj = lambda w: jnp.einsum('sjgpd,gh->jsgdhp', w[::-1].reshape(L, nblk, gb, P, C), eye).reshape(
        nblk, L * V7X_LANES, gb * P)
    ninj = jnp.concatenate([inj(wr), inj(wi)], axis=2)
    o_re = cre[None] * pwr[1:, :, None, :] - cim[None] * pwi[1:, :, None, :]
    o_im = cre[None] * pwi[1:, :, None, :] + cim[None] * pwr[1:, :, None, :]
    out = lambda w: jnp.einsum('tjgcp,gh->jgpthc', w.reshape(L, nblk, gb, C, P), eye).reshape(
        nblk, gb * P, L * V7X_LANES)
    pout = jnp.concatenate([out(o_re), out(-o_im)], axis=1)
    levels = []
    d = 1
    while d < max(rc, 2):
        lr, li = lam_pow(float(L * d))
        levels.append(jnp.concatenate([lr.reshape(nblk, gb * P), li.reshape(nblk, gb * P)], axis=1))
        d *= 2
    lam_tab = jnp.stack(levels, axis=1)
    dvec = d_skip.astype(F32).reshape(nblk, 1, V7X_LANES)
    return dk, ninj.astype(BF16), pout.astype(BF16), lam_tab, dvec


def _ssm(u2, tables, seq, tile):
    n = u2.shape[0]
    L, W = SSM_CHUNK, V7X_LANES
    rc = tile // L
    dk, ninj, pout, lam_tab, dvec = tables
    nblk = dk.shape[0]
    per_blk = lambda a: pl.BlockSpec((1,) + a.shape[1:], lambda j, r: (j,) + (0,) * (a.ndim - 1))
    tok = pl.BlockSpec((tile, W), lambda j, r: (r, j))
    return pl.pallas_call(
        functools.partial(_ssm_body, rc=rc, tiles_per_batch=seq // tile),
        grid=(nblk, n // tile),
        in_specs=[tok] + [per_blk(a) for a in (dk, ninj, pout, lam_tab, dvec)],
        out_specs=tok,
        out_shape=jax.ShapeDtypeStruct(u2.shape, F32),
        scratch_shapes=[pltpu.VMEM((L * W, L * W), BF16),
                        pltpu.VMEM((rc, L * W), BF16),
                        pltpu.VMEM((rc, L * W), BF16),
                        pltpu.VMEM((8, ninj.shape[2]), F32)],
        compiler_params=_params(("parallel", "arbitrary")),
        name="ssm",
    )(u2, dk, ninj, pout, lam_tab, dvec)


def _memkv_body(m_ref, g_ref, w_ref, kv_ref):
    mn = _rms(m_ref[...], g_ref[...]).astype(BF16)
    kv_ref[...] = _dot(mn, w_ref[...]).astype(BF16)


def _mem_kv(mem2, g_mem, w_xkv):
    full = lambda a: pl.BlockSpec(a.shape, lambda i: (0, 0))
    return pl.pallas_call(
        _memkv_body,
        grid=(1,),
        in_specs=[full(mem2), full(g_mem), full(w_xkv)],
        out_specs=pl.BlockSpec((mem2.shape[0], 2 * D_MODEL), lambda i: (0, 0)),
        out_shape=jax.ShapeDtypeStruct((mem2.shape[0], 2 * D_MODEL), BF16),
        compiler_params=_params(("arbitrary",)),
        name="mem_kv",
    )(mem2, g_mem, w_xkv)


def _mix_body(x_ref, a_ref, y_ref, gao_ref, gso_ref, wglu_ref, bglu_ref, wout_ref, gx_ref,
              wxq_ref, kv_ref, wxo_ref, gffn_ref, h2_ref, hn_ref):
    an = _rms(a_ref[0], gao_ref[...]).astype(BF16)
    y = y_ref[0]
    gate = 1.0 / (1.0 + jnp.exp(-(_dot(y.astype(BF16), wglu_ref[...]) + bglu_ref[...])))
    sn = _rms(y * gate, gso_ref[...]).astype(BF16)
    h1 = x_ref[0] + _dot(jnp.concatenate([an, sn], axis=1), wout_ref[...])
    qx = _dot(_rms(h1, gx_ref[...]).astype(BF16), wxq_ref[...])
    heads = []
    for h in range(XATTN_HEADS):
        lo, hi = h * XATTN_HEAD_DIM, (h + 1) * XATTN_HEAD_DIM
        km = kv_ref[0, :, lo:hi]
        vm = kv_ref[0, :, D_MODEL + lo:D_MODEL + hi]
        sc = lax.dot_general(qx[:, lo:hi].astype(BF16), km, NT_DIMS,
                             preferred_element_type=F32) * (XATTN_HEAD_DIM ** -0.5)
        e = jnp.exp(sc - jnp.max(sc, axis=-1, keepdims=True))
        p = e / jnp.sum(e, axis=-1, keepdims=True)
        heads.append(_dot(p.astype(BF16), vm).astype(BF16))
    h2 = h1 + _dot(jnp.concatenate(heads, axis=1), wxo_ref[...])
    h2_ref[0] = h2
    hn_ref[0] = _rms(h2, gffn_ref[...]).astype(BF16)


def _mix(x, attn, yssm, kv, g_attn_out, g_ssm_out, w_glu, b_glu, w_out, g_xattn, w_xq, w_xo, g_ffn, tile):
    b, s, _ = x.shape
    tok = lambda w: pl.BlockSpec((1, tile, w), lambda bi, i: (bi, i, 0))
    full = lambda a: pl.BlockSpec(a.shape, lambda bi, i: (0,) * a.ndim)
    kvspec = pl.BlockSpec((1,) + kv.shape[1:], lambda bi, i: (bi, 0, 0))
    return pl.pallas_call(
        _mix_body,
        grid=(b, s // tile),
        in_specs=[tok(D_MODEL), tok(ATTN_WIDTH), tok(SSM_WIDTH), full(g_attn_out), full(g_ssm_out),
                  full(w_glu), full(b_glu), full(w_out), full(g_xattn), full(w_xq), kvspec,
                  full(w_xo), full(g_ffn)],
        out_specs=[tok(D_MODEL), tok(D_MODEL)],
        out_shape=[jax.ShapeDtypeStruct((b, s, D_MODEL), F32),
                   jax.ShapeDtypeStruct((b, s, D_MODEL), BF16)],
        compiler_params=_params(("parallel", "parallel")),
        name="mix",
    )(x, attn, yssm, g_attn_out, g_ssm_out, w_glu, b_glu, w_out, g_xattn, w_xq, kv, w_xo, g_ffn)


def _topk_rows(s, k, val_ref=None, idx_ref=None):
    n = s.shape[0]
    rowid = lax.broadcasted_iota(jnp.int32, s.shape, 0).astype(F32)
    for r in range(k):
        m = jnp.max(s, axis=0, keepdims=True)
        ix = jnp.min(jnp.where(s == m, rowid, float(n)), axis=0, keepdims=True)
        if val_ref is not None:
            val_ref[r:r + 1, :] = m
            idx_ref[r:r + 1, :] = ix
        s = jnp.where(rowid == ix, -jnp.inf, s)
    return s


_CAND_COLS = [PEER_TOPK // (a + 1) for a in range(PEER_TOPK)]
_CAND_OFFS = [sum(_CAND_COLS[:a]) for a in range(PEER_TOPK)]
_CAND_ROWS = -(-sum(_CAND_COLS) // 8) * 8


def _route_body(hn_ref, wq_ref, sk_ref, an_ref, nn_ref, bn_ref, rk_ref,
                v1_ref, i1_ref, v2_ref, i2_ref, cand_ref):
    k = PEER_TOPK
    qt = lax.dot_general(wq_ref[...], hn_ref[...], NT_DIMS, preferred_element_type=F32)
    t = qt.shape[1]
    keyrow = lax.broadcasted_iota(jnp.int32, (PEER_KEYS, t), 0).astype(F32)
    for h in range(PEER_HEADS):
        scores = []
        for half, (vr, ir) in enumerate(((v1_ref, i1_ref), (v2_ref, i2_ref))):
            j = 2 * h + half
            qh = qt[j * PEER_KEY_DIM:(j + 1) * PEER_KEY_DIM, :].astype(BF16)
            scores.append(_dot(sk_ref[j], qh))
            _topk_rows(scores[half], k, vr, ir)
        v1, v2 = v1_ref[...], v2_ref[...]
        cand_ref[...] = jnp.full(cand_ref.shape, -jnp.inf, F32)
        for a in range(k):
            off, nb = _CAND_OFFS[a], _CAND_COLS[a]
            cand_ref[off:off + nb, :] = v1[a:a + 1, :] + v2[0:nb, :]
        left = _topk_rows(cand_ref[...], k)
        sel = jnp.where(left == -jnp.inf, 1.0, 0.0)
        e1 = jnp.exp(v1 - v1[0:1, :])
        e2 = jnp.exp(v2 - v2[0:1, :])
        counts, z = [], jnp.zeros((1, t), F32)
        for a in range(k):
            off, nb = _CAND_OFFS[a], _CAND_COLS[a]
            sel_a = sel[off:off + nb, :]
            counts.append(jnp.sum(sel_a, axis=0, keepdims=True))
            z = z + e1[a:a + 1, :] * jnp.sum(sel_a * e2[0:nb, :], axis=0, keepdims=True)
        i1, i2 = i1_ref[...], i2_ref[...]
        nn = jnp.zeros((PEER_KEYS, t), F32)
        rk = jnp.full((PEER_KEYS, t), float(k), F32)
        for a in range(k):
            nn = jnp.where(keyrow == i1[a:a + 1, :], counts[a], nn)
            rk = jnp.where(keyrow == i2[a:a + 1, :], float(a), rk)
        an_ref[h] = jnp.exp(scores[0] - v1[0:1, :]) * (1.0 / z)
        nn_ref[h] = nn
        bn_ref[h] = jnp.exp(scores[1] - v2[0:1, :]).astype(BF16)
        rk_ref[h] = rk.astype(BF16)


def _route(hn2, wq_t, sk, tile):
    n = hn2.shape[0]
    k = PEER_TOPK
    tab = pl.BlockSpec((PEER_HEADS, PEER_KEYS, tile), lambda i: (0, 0, i))
    tab_shape = lambda dt: jax.ShapeDtypeStruct((PEER_HEADS, PEER_KEYS, n), dt)
    return pl.pallas_call(
        _route_body,
        grid=(n // tile,),
        in_specs=[pl.BlockSpec((tile, D_MODEL), lambda i: (i, 0)),
                  pl.BlockSpec(wq_t.shape, lambda i: (0, 0)),
                  pl.BlockSpec(sk.shape, lambda i: (0, 0, 0))],
        out_specs=[tab] * 4,
        out_shape=[tab_shape(F32), tab_shape(F32), tab_shape(BF16), tab_shape(BF16)],
        scratch_shapes=[pltpu.VMEM((k, tile), F32)] * 4 + [pltpu.VMEM((_CAND_ROWS, tile), F32)],
        compiler_params=_params(("parallel",)),
        name="route",
    )(hn2, wq_t, sk)


def _peer_body(hn_ref, u_ref, vt_ref, an_ref, nn_ref, bn_ref, rk_ref, h2_ref, gf_ref, o_ref,
               acc_ref, w_ref, act_ref, rows_ref, *, rows):
    e = pl.program_id(1)

    @pl.when(e == 0)
    def _():
        acc_ref[...] = jnp.zeros_like(acc_ref)

    for h in range(PEER_HEADS):
        for rr in range(rows):
            pos = 2 * (h * rows + rr)
            rows_ref[pos:pos + 1, :] = an_ref[h, pl.ds(e * rows + rr, 1), :]
            rows_ref[pos + 1:pos + 2, :] = nn_ref[h, pl.ds(e * rows + rr, 1), :]
    pack = BF16_SUBLANES
    groups = PEER_KEYS // pack
    act_ref[...] = lax.dot_general(u_ref[...], hn_ref[...], NT_DIMS, preferred_element_type=F32)
    for rr in range(rows):
        keys = slice(rr * PEER_KEYS, (rr + 1) * PEER_KEYS)
        for c in range(act_ref.shape[1] // V7X_LANES):
            lanes = slice(c * V7X_LANES, (c + 1) * V7X_LANES)
            gate = None
            for h in range(PEER_HEADS):
                pos = 2 * (h * rows + rr)
                a16 = jnp.broadcast_to(rows_ref[pos:pos + 1, lanes], (pack, V7X_LANES)).astype(BF16)
                n16 = jnp.broadcast_to(rows_ref[pos + 1:pos + 2, lanes], (pack, V7X_LANES)).astype(BF16)
                rk3 = rk_ref[h, :, lanes].reshape(groups, pack, V7X_LANES)
                bn3 = bn_ref[h, :, lanes].reshape(groups, pack, V7X_LANES)
                term = a16[None] * jnp.where(rk3 < n16[None], bn3, jnp.zeros_like(bn3))
                gate = term if gate is None else gate + term
            act = act_ref[keys, lanes]
            w_ref[keys, lanes] = jax.nn.gelu(act).astype(BF16) * gate.reshape(PEER_KEYS, V7X_LANES)
    acc_ref[...] += _dot(vt_ref[...], w_ref[...])

    @pl.when(e == pl.num_programs(1) - 1)
    def _():
        h3 = h2_ref[...] + acc_ref[...].T
        o_ref[...] = _rms(h3, gf_ref[...])


def _peer(hn2, pu, pvt, tables, h2, g_final, tile, rows):
    n = hn2.shape[0]
    te = rows * PEER_KEYS
    n_tiles = pu.shape[0] // te
    tok = pl.BlockSpec((tile, D_MODEL), lambda i, e: (i, 0))
    tab = pl.BlockSpec((PEER_HEADS, PEER_KEYS, tile), lambda i, e: (0, 0, i))
    return pl.pallas_call(
        functools.partial(_peer_body, rows=rows),
        grid=(n // tile, n_tiles),
        in_specs=[tok,
                  pl.BlockSpec((te, D_MODEL), lambda i, e: (e, 0)),
                  pl.BlockSpec((D_MODEL, te), lambda i, e: (0, e)),
                  tab, tab, tab, tab, tok,
                  pl.BlockSpec(g_final.shape, lambda i, e: (0, 0))],
        out_specs=tok,
        out_shape=jax.ShapeDtypeStruct((n, D_MODEL), F32),
        scratch_shapes=[pltpu.VMEM((D_MODEL, tile), F32),
                        pltpu.VMEM((te, tile), BF16),
                        pltpu.VMEM((te, tile), F32),
                        pltpu.VMEM((2 * PEER_HEADS * rows, tile), F32)],
        compiler_params=_params(("parallel", "arbitrary")),
        name="peer",
    )(hn2, pu, pvt, *tables, h2, g_final)


TOKEN_TILE = 512
ATTN_BLOCK = 256
ATTN_HEADS_PER_STEP = ATTN_BLOCK // HEAD_DIM
PEER_ROWS = 8
SSM_TILE = 4096


def kernel(x, mem, g_mix, w_in, a_re, a_im, log_dt, b_re, b_im, c_re, c_im, d_skip, w_glu, b_glu,
           g_attn_out, g_ssm_out, w_out, g_xattn, g_mem, w_xq, w_xkv, w_xo, g_ffn, w_pq, sub_keys,
           peer_u, peer_v, g_final):
    b, s, d = x.shape
    n = b * s
    depth = g_mix.shape[0]
    row = lambda v: v.reshape(1, -1).astype(F32)
    h = x
    for layer in range(depth):
        q, k, v, u = _in_proj(h.reshape(n, d), row(g_mix[layer]), w_in[layer].astype(BF16), TOKEN_TILE)
        attn = _attention(q.reshape(b, s, -1), k.reshape(b, s, -1), v.reshape(b, s, -1), ATTN_BLOCK,
                          ATTN_HEADS_PER_STEP)
        ssm_tile = min(SSM_TILE, s)
        tables = _ssm_tables(a_re[layer], a_im[layer], log_dt[layer], b_re[layer], b_im[layer],
                             c_re[layer], c_im[layer], d_skip[layer], ssm_tile // SSM_CHUNK)
        yssm = _ssm(u, tables, s, ssm_tile).reshape(b, s, -1)
        kv = _mem_kv(mem.reshape(-1, d), row(g_mem[layer]), w_xkv[layer].astype(BF16))
        h2, hn = _mix(h, attn, yssm, kv.reshape(b, -1, 2 * d), row(g_attn_out[layer]), row(g_ssm_out[layer]),
                      w_glu[layer].astype(BF16), row(b_glu[layer]), w_out[layer].astype(BF16),
                      row(g_xattn[layer]), w_xq[layer].astype(BF16), w_xo[layer].astype(BF16),
                      row(g_ffn[layer]), TOKEN_TILE)
        hn2 = hn.reshape(n, d)
        sk = sub_keys[layer].reshape(2 * PEER_HEADS, PEER_KEYS, PEER_KEY_DIM).astype(BF16)
        routing = _route(hn2, w_pq[layer].T.astype(BF16), sk, TOKEN_TILE)
        assert depth == 1
        out = _peer(hn2, peer_u[layer].astype(BF16), peer_v[layer].T.astype(BF16), routing,
                    h2.reshape(n, d), row(g_final), TOKEN_TILE, PEER_ROWS)
        h = out.reshape(b, s, d)
    return h
```

```python
import functools
import math

import jax
import jax.numpy as jnp
from jax import lax
from jax.experimental import pallas as pl
from jax.experimental.pallas import tpu as pltpu

F32 = jnp.float32
BF16 = jnp.bfloat16

EPS = 1e-6
D_MODEL = 1024
ATTN_HEADS = 8
HEAD_DIM = 64
ATTN_WIDTH = ATTN_HEADS * HEAD_DIM
SSM_WIDTH = D_MODEL - ATTN_WIDTH
SSM_GROUP = 16
SSM_GROUPS = SSM_WIDTH // SSM_GROUP
SSM_STATE = 64
SSM_CHUNK = 16
SSM_PAIRS = SSM_GROUPS // 2
XATTN_HEADS = 4
XATTN_HEAD_DIM = D_MODEL // XATTN_HEADS
PEER_HEADS = 8
PEER_KEYS = 128
PEER_TOPK = 16
PEER_KEY_DIM = 128

V7X_LANES = 128
BF16_SUBLANES = 16
V7X_VMEM_LIMIT_BYTES = 56 * 1024 * 1024

NT_DIMS = (((1,), (1,)), ((), ()))
LOG2_E = math.log2(math.e)


def _params(semantics):
    return pltpu.CompilerParams(dimension_semantics=semantics,
                                vmem_limit_bytes=V7X_VMEM_LIMIT_BYTES)


def _rms(x, g):
    ms = jnp.mean(x * x, axis=-1, keepdims=True)
    return x * lax.rsqrt(ms + EPS) * g


def _dot(a, b):
    return jnp.dot(a, b, preferred_element_type=F32)


def _split(a):
    hi = a.astype(BF16)
    lo = (a - hi.astype(F32)).astype(BF16)
    return hi, lo


def _dot3(a, b):
    ah, al = _split(a)
    bh, bl = _split(b)
    return _dot(ah, bh) + _dot(al, bh) + _dot(ah, bl)


def _inproj_body(x_ref, g_ref, w_ref, q_ref, k_ref, v_ref, u_ref):
    xn = _rms(x_ref[...], g_ref[...]).astype(BF16)
    proj = _dot(xn, w_ref[...])
    aw = ATTN_WIDTH
    q_ref[...] = (proj[:, 0:aw] * (HEAD_DIM ** -0.5)).astype(BF16)
    k_ref[...] = proj[:, aw:2 * aw].astype(BF16)
    v_ref[...] = proj[:, 2 * aw:3 * aw].astype(BF16)
    u_ref[...] = proj[:, 3 * aw:]


def _in_proj(x2, g_mix, w_in, tile):
    n = x2.shape[0]
    tok = lambda w: pl.BlockSpec((tile, w), lambda i: (i, 0))
    full = lambda a: pl.BlockSpec(a.shape, lambda i: (0, 0))
    return pl.pallas_call(
        _inproj_body,
        grid=(n // tile,),
        in_specs=[tok(D_MODEL), full(g_mix), full(w_in)],
        out_specs=[tok(ATTN_WIDTH)] * 3 + [tok(SSM_WIDTH)],
        out_shape=[jax.ShapeDtypeStruct((n, ATTN_WIDTH), BF16)] * 3
        + [jax.ShapeDtypeStruct((n, SSM_WIDTH), F32)],
        compiler_params=_params(("parallel",)),
        name="in_proj",
    )(x2, g_mix, w_in)


def _attn_body(q_ref, k_ref, v_ref, o_ref, acc_ref, car_ref, z_ref, w_ref, *, blk, heads):
    i = pl.program_id(2)
    width = heads * HEAD_DIM
    qa = q_ref[0]
    lane = lax.broadcasted_iota(jnp.int32, (blk, width), 1)
    in_head = [(lane >= h * HEAD_DIM) & (lane < (h + 1) * HEAD_DIM) for h in range(heads)]
    zero = jnp.zeros_like(qa)
    q_heads = [jnp.where(m, qa, zero) for m in in_head]
    row = lax.broadcasted_iota(jnp.int32, (blk, blk), 0)
    col = lax.broadcasted_iota(jnp.int32, (blk, blk), 1)
    tri = jnp.where(row > col, 1.0, 0.0).astype(BF16)
    causal = col < row

    acc_ref[...] = jnp.zeros_like(acc_ref)
    car_ref[...] = jnp.zeros_like(car_ref)

    def scores(j):
        kb = k_ref[0, pl.ds(pl.multiple_of(j * blk, blk), blk), :]
        for h in range(heads):
            z_ref[h] = lax.dot_general(q_heads[h], kb, NT_DIMS, preferred_element_type=F32)

    def weights(mask):
        for h in range(heads):
            z = z_ref[h]
            sp = jnp.maximum(z, 0.0) + jnp.log(1.0 + jnp.exp2(jnp.abs(z) * -LOG2_E))
            if mask is not None:
                sp = jnp.where(mask, sp, 0.0)
            inner = _dot(sp.astype(BF16), tri)
            carry = car_ref[h]
            later = inner + jnp.concatenate([carry] * (blk // V7X_LANES), axis=1)
            w = jnp.exp2((z - sp - later) * LOG2_E)
            if mask is not None:
                w = jnp.where(mask, w, 0.0)
            w_ref[:, h * blk:(h + 1) * blk] = w.astype(BF16)
            total = inner[:, 0:1] + sp[:, 0:1]
            car_ref[h] = carry + jnp.broadcast_to(total, carry.shape)

    def values(j):
        vb = v_ref[0, pl.ds(pl.multiple_of(j * blk, blk), blk), :]
        vs = [jnp.where(in_head[h], vb, jnp.zeros_like(vb)) for h in range(heads)]
        acc_ref[...] += _dot(w_ref[...], jnp.concatenate(vs, axis=0))

    scores(i)
    weights(causal)
    scores(jnp.maximum(i - 1, 0))

    def body(t, _):
        j = i - 1 - t
        values(j + 1)
        weights(None)
        scores(jnp.maximum(j - 1, 0))
        return 0

    lax.fori_loop(0, i, body, 0)
    values(0)
    o_ref[0] = acc_ref[...]


def _attention(q, k, v, blk, heads):
    b, s, _ = q.shape
    width = heads * HEAD_DIM
    assert blk == width, "the value masks reuse the (blk, width) lane iota"
    qspec = pl.BlockSpec((1, blk, width), lambda bi, p, i: (bi, i, p))
    kvspec = pl.BlockSpec((1, s, width), lambda bi, p, i: (bi, 0, p))
    return pl.pallas_call(
        functools.partial(_attn_body, blk=blk, heads=heads),
        grid=(b, ATTN_WIDTH // width, s // blk),
        in_specs=[qspec, kvspec, kvspec],
        out_specs=qspec,
        out_shape=jax.ShapeDtypeStruct((b, s, ATTN_WIDTH), F32),
        scratch_shapes=[pltpu.VMEM((blk, width), F32),
                        pltpu.VMEM((heads, blk, V7X_LANES), F32),
                        pltpu.VMEM((heads, blk, blk), F32),
                        pltpu.VMEM((blk, heads * blk), BF16)],
        compiler_params=_params(("parallel", "parallel", "arbitrary")),
        name="attn",
    )(q, k, v)


def _ssm_body(x_ref, dk_ref, ninj_ref, pout_ref, lam_ref, d_ref, y_ref, m_ref, xh_ref, xl_ref, car_ref,
              *, rc, tiles_per_batch):
    r = pl.program_id(1)
    L, W = SSM_CHUNK, V7X_LANES
    half = (W // SSM_GROUP) * SSM_STATE

    @pl.when(r == 0)
    def _():
        m_ref[...] = jnp.zeros_like(m_ref)
        for lag in range(L):
            dk = dk_ref[0, lag].astype(BF16)
            for s in range(L - lag):
                t = s + lag
                m_ref[s * W:(s + 1) * W, t * W:(t + 1) * W] = dk

    @pl.when(r % tiles_per_batch == 0)
    def _():
        car_ref[...] = jnp.zeros_like(car_ref)

    for s in range(L):
        hi, lo = _split(x_ref[pl.ds(s, rc, stride=L), :])
        xh_ref[:, s * W:(s + 1) * W] = hi
        xl_ref[:, s * W:(s + 1) * W] = lo

    def dot2(ah, al, w):
        return _dot(ah, w) + _dot(al, w)

    s_all = dot2(xh_ref[...], xl_ref[...], ninj_ref[0])
    sr, si = s_all[:, :half], s_all[:, half:]
    row = lax.broadcasted_iota(jnp.int32, (rc, half), 0)
    cr, ci = car_ref[0:1, :half], car_ref[0:1, half:]
    ar, ai = lam_ref[0, 0:1, :half], lam_ref[0, 0:1, half:]
    first = row == 0
    sr = sr + jnp.where(first, ar * cr - ai * ci, 0.0)
    si = si + jnp.where(first, ar * ci + ai * cr, 0.0)
    d, lvl = 1, 0
    while d < rc:
        ar, ai = lam_ref[0, lvl:lvl + 1, :half], lam_ref[0, lvl:lvl + 1, half:]
        keep = row >= d
        pr = jnp.where(keep, pltpu.roll(sr, d, 0), 0.0)
        pi = jnp.where(keep, pltpu.roll(si, d, 0), 0.0)
        sr, si = sr + ar * pr - ai * pi, si + ar * pi + ai * pr
        d, lvl = 2 * d, lvl + 1
    prev = jnp.concatenate([jnp.where(first, cr, pltpu.roll(sr, 1, 0)),
                            jnp.where(first, ci, pltpu.roll(si, 1, 0))], axis=1)
    car_ref[0:1, :half] = sr[rc - 1:rc, :]
    car_ref[0:1, half:] = si[rc - 1:rc, :]
    ph, plo = _split(prev)
    far = dot2(ph, plo, pout_ref[0])
    for jt in range(L // 2):
        kk = (2 * jt + 2) * W
        cols = slice(2 * jt * W, (2 * jt + 2) * W)
        yt = dot2(xh_ref[:, :kk], xl_ref[:, :kk], m_ref[:kk, cols]) + far[:, cols]
        for t in (2 * jt, 2 * jt + 1):
            yv = yt[:, (t - 2 * jt) * W:(t - 2 * jt + 1) * W] + d_ref[0] * x_ref[pl.ds(t, rc, stride=L), :]
            y_ref[pl.ds(t, rc, stride=L), :] = jax.nn.gelu(yv)


def _ssm_tables(a_re, a_im, log_dt, b_re, b_im, c_re, c_im, d_skip, rc):
    L, G, P, C = SSM_CHUNK, SSM_GROUPS, SSM_STATE, SSM_GROUP
    gb = V7X_LANES // C
    nblk = G // gb
    ar, ai = a_re.astype(F32), a_im.astype(F32)
    dt = jnp.exp(log_dt.astype(F32))[:, None]
    ldr, ldi = ar * dt, ai * dt

    def lam_pow(k):
        mag = jnp.exp(k * ldr)
        return mag * jnp.cos(k * ldi), mag * jnp.sin(k * ldi)

    lbr, lbi = lam_pow(1.0)
    den = ar * ar + ai * ai
    qr = ((lbr - 1.0) * ar + lbi * ai) / den
    qi = (lbi * ar - (lbr - 1.0) * ai) / den
    br, bi = b_re.astype(F32), b_im.astype(F32)
    bbr = qr[..., None] * br - qi[..., None] * bi
    bbi = qr[..., None] * bi + qi[..., None] * br
    cre, cim = c_re.astype(F32), c_im.astype(F32)
    steps = jnp.arange(L + 1, dtype=F32)[:, None, None]
    pwr, pwi = lam_pow(steps)
    wr = pwr[:L, :, :, None] * bbr[None] - pwi[:L, :, :, None] * bbi[None]
    wi = pwr[:L, :, :, None] * bbi[None] + pwi[:L, :, :, None] * bbr[None]
    kern = jnp.einsum('gcp,kgpd->kgcd', cre, wr) - jnp.einsum('gcp,kgpd->kgcd', cim, wi)
    eye = jnp.eye(gb, dtype=F32)
    dk = jnp.einsum('kjgcd,gh->jkgdhc', kern.reshape(L, nblk, gb, C, C), eye).reshape(nblk, L, V7X_LANES, V7X_LANES)
    inj = lambda w: jnp.einsum('sjgpd,gh->jsgdhp', w[::-1].reshape(L, nblk, gb, P, C), eye).reshape(
        nblk, L * V7X_LANES, gb * P)
    ninj = jnp.concatenate([inj(wr), inj(wi)], axis=2)
    o_re = cre[None] * pwr[1:, :, None, :] - cim[None] * pwi[1:, :, None, :]
    o_im = cre[None] * pwi[1:, :, None, :] + cim[None] * pwr[1:, :, None, :]
    out = lambda w: jnp.einsum('tjgcp,gh->jgpthc', w.reshape(L, nblk, gb, C, P), eye).reshape(
        nblk, gb * P, L * V7X_LANES)
    pout = jnp.concatenate([out(o_re), out(-o_im)], axis=1)
    levels = []
    d = 1
    while d < max(rc, 2):
        lr, li = lam_pow(float(L * d))
        levels.append(jnp.concatenate([lr.reshape(nblk, gb * P), li.reshape(nblk, gb * P)], axis=1))
        d *= 2
    lam_tab = jnp.stack(levels, axis=1)
    dvec = d_skip.astype(F32).reshape(nblk, 1, V7X_LANES)
    return dk, ninj.astype(BF16), pout.astype(BF16), lam_tab, dvec


def _ssm(u2, tables, seq, tile):
    n = u2.shape[0]
    L, W = SSM_CHUNK, V7X_LANES
    rc = tile // L
    dk, ninj, pout, lam_tab, dvec = tables
    nblk = dk.shape[0]
    per_blk = lambda a: pl.BlockSpec((1,) + a.shape[1:], lambda j, r: (j,) + (0,) * (a.ndim - 1))
    tok = pl.BlockSpec((tile, W), lambda j, r: (r, j))
    return pl.pallas_call(
        functools.partial(_ssm_body, rc=rc, tiles_per_batch=seq // tile),
        grid=(nblk, n // tile),
        in_specs=[tok] + [per_blk(a) for a in (dk, ninj, pout, lam_tab, dvec)],
        out_specs=tok,
        out_shape=jax.ShapeDtypeStruct(u2.shape, F32),
        scratch_shapes=[pltpu.VMEM((L * W, L * W), BF16),
                        pltpu.VMEM((rc, L * W), BF16),
                        pltpu.VMEM((rc, L * W), BF16),
                        pltpu.VMEM((8, ninj.shape[2]), F32)],
        compiler_params=_params(("parallel", "arbitrary")),
        name="ssm",
    )(u2, dk, ninj, pout, lam_tab, dvec)


def _memkv_body(m_ref, g_ref, w_ref, kv_ref):
    mn = _rms(m_ref[...], g_ref[...]).astype(BF16)
    kv_ref[...] = _dot(mn, w_ref[...]).astype(BF16)


def _mem_kv(mem2, g_mem, w_xkv):
    full = lambda a: pl.BlockSpec(a.shape, lambda i: (0, 0))
    return pl.pallas_call(
        _memkv_body,
        grid=(1,),
        in_specs=[full(mem2), full(g_mem), full(w_xkv)],
        out_specs=pl.BlockSpec((mem2.shape[0], 2 * D_MODEL), lambda i: (0, 0)),
        out_shape=jax.ShapeDtypeStruct((mem2.shape[0], 2 * D_MODEL), BF16),
        compiler_params=_params(("arbitrary",)),
        name="mem_kv",
    )(mem2, g_mem, w_xkv)


def _mix_body(x_ref, a_ref, y_ref, gao_ref, gso_ref, wglu_ref, bglu_ref, wout_ref, gx_ref,
              wxq_ref, kv_ref, wxo_ref, gffn_ref, h2_ref, hn_ref):
    an = _rms(a_ref[0], gao_ref[...]).astype(BF16)
    y = y_ref[0]
    gate = 1.0 / (1.0 + jnp.exp(-(_dot(y.astype(BF16), wglu_ref[...]) + bglu_ref[...])))
    sn = _rms(y * gate, gso_ref[...]).astype(BF16)
    h1 = x_ref[0] + _dot(jnp.concatenate([an, sn], axis=1), wout_ref[...])
    qx = _dot(_rms(h1, gx_ref[...]).astype(BF16), wxq_ref[...])
    heads = []
    for h in range(XATTN_HEADS):
        lo, hi = h * XATTN_HEAD_DIM, (h + 1) * XATTN_HEAD_DIM
        km = kv_ref[0, :, lo:hi]
        vm = kv_ref[0, :, D_MODEL + lo:D_MODEL + hi]
        sc = lax.dot_general(qx[:, lo:hi].astype(BF16), km, NT_DIMS,
                             preferred_element_type=F32) * (XATTN_HEAD_DIM ** -0.5)
        e = jnp.exp(sc - jnp.max(sc, axis=-1, keepdims=True))
        p = e / jnp.sum(e, axis=-1, keepdims=True)
        heads.append(_dot(p.astype(BF16), vm).astype(BF16))
    h2 = h1 + _dot(jnp.concatenate(heads, axis=1), wxo_ref[...])
    h2_ref[0] = h2
    hn_ref[0] = _rms(h2, gffn_ref[...]).astype(BF16)


def _mix(x, attn, yssm, kv, g_attn_out, g_ssm_out, w_glu, b_glu, w_out, g_xattn, w_xq, w_xo, g_ffn, tile):
    b, s, _ = x.shape
    tok = lambda w: pl.BlockSpec((1, tile, w), lambda bi, i: (bi, i, 0))
    full = lambda a: pl.BlockSpec(a.shape, lambda bi, i: (0,) * a.ndim)
    kvspec = pl.BlockSpec((1,) + kv.shape[1:], lambda bi, i: (bi, 0, 0))
    return pl.pallas_call(
        _mix_body,
        grid=(b, s // tile),
        in_specs=[tok(D_MODEL), tok(ATTN_WIDTH), tok(SSM_WIDTH), full(g_attn_out), full(g_ssm_out),
                  full(w_glu), full(b_glu), full(w_out), full(g_xattn), full(w_xq), kvspec,
                  full(w_xo), full(g_ffn)],
        out_specs=[tok(D_MODEL), tok(D_MODEL)],
        out_shape=[jax.ShapeDtypeStruct((b, s, D_MODEL), F32),
                   jax.ShapeDtypeStruct((b, s, D_MODEL), BF16)],
        compiler_params=_params(("parallel", "parallel")),
        name="mix",
    )(x, attn, yssm, g_attn_out, g_ssm_out, w_glu, b_glu, w_out, g_xattn, w_xq, kv, w_xo, g_ffn)


def _topk_rows(s, k, val_ref=None, idx_ref=None):
    n = s.shape[0]
    rowid = lax.broadcasted_iota(jnp.int32, s.shape, 0).astype(F32)
    for r in range(k):
        m = jnp.max(s, axis=0, keepdims=True)
        ix = jnp.min(jnp.where(s == m, rowid, float(n)), axis=0, keepdims=True)
        if val_ref is not None:
            val_ref[r:r + 1, :] = m
            idx_ref[r:r + 1, :] = ix
        s = jnp.where(rowid == ix, -jnp.inf, s)
    return s


_CAND_COLS = [PEER_TOPK // (a + 1) for a in range(PEER_TOPK)]
_CAND_OFFS = [sum(_CAND_COLS[:a]) for a in range(PEER_TOPK)]
_CAND_ROWS = -(-sum(_CAND_COLS) // 8) * 8


def _route_body(hn_ref, wq_ref, sk_ref, an_ref, nn_ref, bn_ref, rk_ref,
                v1_ref, i1_ref, v2_ref, i2_ref, cand_ref):
    k = PEER_TOPK
    qt = lax.dot_general(wq_ref[...], hn_ref[...], NT_DIMS, preferred_element_type=F32)
    t = qt.shape[1]
    keyrow = lax.broadcasted_iota(jnp.int32, (PEER_KEYS, t), 0).astype(F32)
    for h in range(PEER_HEADS):
        scores = []
        for half, (vr, ir) in enumerate(((v1_ref, i1_ref), (v2_ref, i2_ref))):
            j = 2 * h + half
            qh = qt[j * PEER_KEY_DIM:(j + 1) * PEER_KEY_DIM, :].astype(BF16)
            scores.append(_dot(sk_ref[j], qh))
            _topk_rows(scores[half], k, vr, ir)
        v1, v2 = v1_ref[...], v2_ref[...]
        cand_ref[...] = jnp.full(cand_ref.shape, -jnp.inf, F32)
        for a in range(k):
            off, nb = _CAND_OFFS[a], _CAND_COLS[a]
            cand_ref[off:off + nb, :] = v1[a:a + 1, :] + v2[0:nb, :]
        left = _topk_rows(cand_ref[...], k)
        sel = jnp.where(left == -jnp.inf, 1.0, 0.0)
        e1 = jnp.exp(v1 - v1[0:1, :])
        e2 = jnp.exp(v2 - v2[0:1, :])
        counts, z = [], jnp.zeros((1, t), F32)
        for a in range(k):
            off, nb = _CAND_OFFS[a], _CAND_COLS[a]
            sel_a = sel[off:off + nb, :]
            counts.append(jnp.sum(sel_a, axis=0, keepdims=True))
            z = z + e1[a:a + 1, :] * jnp.sum(sel_a * e2[0:nb, :], axis=0, keepdims=True)
        i1, i2 = i1_ref[...], i2_ref[...]
        nn = jnp.zeros((PEER_KEYS, t), F32)
        rk = jnp.full((PEER_KEYS, t), float(k), F32)
        for a in range(k):
            nn = jnp.where(keyrow == i1[a:a + 1, :], counts[a], nn)
            rk = jnp.where(keyrow == i2[a:a + 1, :], float(a), rk)
        an_ref[h] = jnp.exp(scores[0] - v1[0:1, :]) * (1.0 / z)
        nn_ref[h] = nn
        bn_ref[h] = jnp.exp(scores[1] - v2[0:1, :]).astype(BF16)
        rk_ref[h] = rk.astype(BF16)


def _route(hn2, wq_t, sk, tile):
    n = hn2.shape[0]
    k = PEER_TOPK
    tab = pl.BlockSpec((PEER_HEADS, PEER_KEYS, tile), lambda i: (0, 0, i))
    tab_shape = lambda dt: jax.ShapeDtypeStruct((PEER_HEADS, PEER_KEYS, n), dt)
    return pl.pallas_call(
        _route_body,
        grid=(n // tile,),
        in_specs=[pl.BlockSpec((tile, D_MODEL), lambda i: (i, 0)),
                  pl.BlockSpec(wq_t.shape, lambda i: (0, 0)),
                  pl.BlockSpec(sk.shape, lambda i: (0, 0, 0))],
        out_specs=[tab] * 4,
        out_shape=[tab_shape(F32), tab_shape(F32), tab_shape(BF16), tab_shape(BF16)],
        scratch_shapes=[pltpu.VMEM((k, tile), F32)] * 4 + [pltpu.VMEM((_CAND_ROWS, tile), F32)],
        compiler_params=_params(("parallel",)),
        name="route",
    )(hn2, wq_t, sk)


def _peer_body(hn_ref, u_ref, vt_ref, an_ref, nn_ref, bn_ref, rk_ref, h2_ref, gf_ref, o_ref,
               acc_ref, w_ref, act_ref, rows_ref, *, rows):
    e = pl.program_id(1)

    @pl.when(e == 0)
    def _():
        acc_ref[...] = jnp.zeros_like(acc_ref)

    for h in range(PEER_HEADS):
        for rr in range(rows):
            pos = 2 * (h * rows + rr)
            rows_ref[pos:pos + 1, :] = an_ref[h, pl.ds(e * rows + rr, 1), :]
            rows_ref[pos + 1:pos + 2, :] = nn_ref[h, pl.ds(e * rows + rr, 1), :]
    pack = BF16_SUBLANES
    groups = PEER_KEYS // pack
    act_ref[...] = lax.dot_general(u_ref[...], hn_ref[...], NT_DIMS, preferred_element_type=F32)
    for rr in range(rows):
        keys = slice(rr * PEER_KEYS, (rr + 1) * PEER_KEYS)
        for c in range(act_ref.shape[1] // V7X_LANES):
            lanes = slice(c * V7X_LANES, (c + 1) * V7X_LANES)
            gate = None
            for h in range(PEER_HEADS):
                pos = 2 * (h * rows + rr)
                a16 = jnp.broadcast_to(rows_ref[pos:pos + 1, lanes], (pack, V7X_LANES)).astype(BF16)
                n16 = jnp.broadcast_to(rows_ref[pos + 1:pos + 2, lanes], (pack, V7X_LANES)).astype(BF16)
                rk3 = rk_ref[h, :, lanes].reshape(groups, pack, V7X_LANES)
                bn3 = bn_ref[h, :, lanes].reshape(groups, pack, V7X_LANES)
                term = a16[None] * jnp.where(rk3 < n16[None], bn3, jnp.zeros_like(bn3))
                gate = term if gate is None else gate + term
            act = act_ref[keys, lanes]
            w_ref[keys, lanes] = jax.nn.gelu(act).astype(BF16) * gate.reshape(PEER_KEYS, V7X_LANES)
    acc_ref[...] += _dot(vt_ref[...], w_ref[...])

    @pl.when(e == pl.num_programs(1) - 1)
    def _():
        h3 = h2_ref[...] + acc_ref[...].T
        o_ref[...] = _rms(h3, gf_ref[...])


def _peer(hn2, pu, pvt, tables, h2, g_final, tile, rows):
    n = hn2.shape[0]
    te = rows * PEER_KEYS
    n_tiles = pu.shape[0] // te
    tok = pl.BlockSpec((tile, D_MODEL), lambda i, e: (i, 0))
    tab = pl.BlockSpec((PEER_HEADS, PEER_KEYS, tile), lambda i, e: (0, 0, i))
    return pl.pallas_call(
        functools.partial(_peer_body, rows=rows),
        grid=(n // tile, n_tiles),
        in_specs=[tok,
                  pl.BlockSpec((te, D_MODEL), lambda i, e: (e, 0)),
                  pl.BlockSpec((D_MODEL, te), lambda i, e: (0, e)),
                  tab, tab, tab, tab, tok,
                  pl.BlockSpec(g_final.shape, lambda i, e: (0, 0))],
        out_specs=tok,
        out_shape=jax.ShapeDtypeStruct((n, D_MODEL), F32),
        scratch_shapes=[pltpu.VMEM((D_MODEL, tile), F32),
                        pltpu.VMEM((te, tile), BF16),
                        pltpu.VMEM((te, tile), F32),
                        pltpu.VMEM((2 * PEER_HEADS * rows, tile), F32)],
        compiler_params=_params(("parallel", "arbitrary")),
        name="peer",
    )(hn2, pu, pvt, *tables, h2, g_final)


TOKEN_TILE = 512
ATTN_BLOCK = 256
ATTN_HEADS_PER_STEP = ATTN_BLOCK // HEAD_DIM
PEER_ROWS = 16
SSM_TILE = 4096


def kernel(x, mem, g_mix, w_in, a_re, a_im, log_dt, b_re, b_im, c_re, c_im, d_skip, w_glu, b_glu,
           g_attn_out, g_ssm_out, w_out, g_xattn, g_mem, w_xq, w_xkv, w_xo, g_ffn, w_pq, sub_keys,
           peer_u, peer_v, g_final):
    b, s, d = x.shape
    n = b * s
    depth = g_mix.shape[0]
    row = lambda v: v.reshape(1, -1).astype(F32)
    h = x
    for layer in range(depth):
        q, k, v, u = _in_proj(h.reshape(n, d), row(g_mix[layer]), w_in[layer].astype(BF16), TOKEN_TILE)
        attn = _attention(q.reshape(b, s, -1), k.reshape(b, s, -1), v.reshape(b, s, -1), ATTN_BLOCK,
                          ATTN_HEADS_PER_STEP)
        ssm_tile = min(SSM_TILE, s)
        tables = _ssm_tables(a_re[layer], a_im[layer], log_dt[layer], b_re[layer], b_im[layer],
                             c_re[layer], c_im[layer], d_skip[layer], ssm_tile // SSM_CHUNK)
        yssm = _ssm(u, tables, s, ssm_tile).reshape(b, s, -1)
        kv = _mem_kv(mem.reshape(-1, d), row(g_mem[layer]), w_xkv[layer].astype(BF16))
        h2, hn = _mix(h, attn, yssm, kv.reshape(b, -1, 2 * d), row(g_attn_out[layer]), row(g_ssm_out[layer]),
                      w_glu[layer].astype(BF16), row(b_glu[layer]), w_out[layer].astype(BF16),
                      row(g_xattn[layer]), w_xq[layer].astype(BF16), w_xo[layer].astype(BF16),
                      row(g_ffn[layer]), TOKEN_TILE)
        hn2 = hn.reshape(n, d)
        sk = sub_keys[layer].reshape(2 * PEER_HEADS, PEER_KEYS, PEER_KEY_DIM).astype(BF16)
        routing = _route(hn2, w_pq[layer].T.astype(BF16), sk, TOKEN_TILE)
        assert depth == 1
        out = _peer(hn2, peer_u[layer].astype(BF16), peer_v[layer].T.astype(BF16), routing,
                    h2.reshape(n, d), row(g_final), TOKEN_TILE, PEER_ROWS)
        h = out.reshape(b, s, d)
    return h
```
